```python
import math
import jax
import jax.numpy as jnp
from jax import lax
import numpy as np

D_MODEL = 1024
BATCH = 8
SEQ = 8192
DEPTH = 4
DEC_BATCH = 16
DEC_SEQ = 16
PAST_LEN = 1024

CHUNK = 64
HEAD_DIM = 64
FA_HEADS = D_MODEL // 128
SGU_GROUPS = D_MODEL // 256
RET_HEADS = D_MODEL // 256
FA_WIDTH = FA_HEADS * HEAD_DIM
SGU_GROUP_DIM = HEAD_DIM
SGU_WIDTH = SGU_GROUPS * SGU_GROUP_DIM
RET_WIDTH = RET_HEADS * HEAD_DIM
MIX_WIDTH = FA_WIDTH + SGU_WIDTH + RET_WIDTH
PROJ_SPLIT_SIZES = (FA_WIDTH, FA_WIDTH, FA_WIDTH, FA_HEADS, SGU_WIDTH, SGU_WIDTH, RET_WIDTH, RET_WIDTH, RET_WIDTH, RET_WIDTH)
PROJ_WIDTH = 3 * FA_WIDTH + FA_HEADS + 2 * SGU_WIDTH + 4 * RET_WIDTH
Q_BLOCK = 128
SGU_CHUNK = 128
DENSE_FF = 256 * ((8 * D_MODEL // 3 + 255) // 256)
N_EXPERTS = 8
TOP_K = 2
EXPERT_FF = 7 * D_MODEL // 2
N_DENSE = (DEPTH + 1) // 2
N_MOE = DEPTH // 2
PLE_DIM = 256
ROPE_BASE = 10000.0
RET_DECAY_MIN = 1.0 / 32.0
RET_DECAY_MAX = 1.0 / 512.0
FORGET_BIAS_INIT = 2.0
NORM_EPS = 1e-6
GN_EPS = 1e-5
NEG_INF = -1e30

kernel_name = 'hybrid_fox_sgu_retention_stream_step'


def rms_norm(x, g):
    xf = x.astype(jnp.float32)
    y = xf * lax.rsqrt(jnp.mean(xf * xf, axis=-1, keepdims=True) + NORM_EPS)
    return (y * g.astype(jnp.float32)).astype(x.dtype)


def layer_norm(x, g, b):
    xf = x.astype(jnp.float32)
    mu = jnp.mean(xf, axis=-1, keepdims=True)
    var = jnp.mean(jnp.square(xf - mu), axis=-1, keepdims=True)
    y = (xf - mu) * lax.rsqrt(var + GN_EPS) * g.astype(jnp.float32) + b.astype(jnp.float32)
    return y.astype(x.dtype)


def rotary(x, pos):
    half = x.shape[-1] // 2
    inv = ROPE_BASE ** (-jnp.arange(half, dtype=jnp.float32) / half)
    ang = pos.astype(jnp.float32)[:, None] * inv[None, :]
    cos = jnp.cos(ang)[None, :, None, :]
    sin = jnp.sin(ang)[None, :, None, :]
    xf = x.astype(jnp.float32)
    x1, x2 = xf[..., :half], xf[..., half:]
    return jnp.concatenate([x1 * cos - x2 * sin, x1 * sin + x2 * cos], axis=-1)


def split_columns(z):
    parts, o = [], 0
    for s in PROJ_SPLIT_SIZES:
        parts.append(z[..., o:o + s])
        o += s
    return parts


def fox_attend(q, k, v, fq, fk, q_pos, k_pos):
    s = jnp.einsum('bqhd,bkhd->bhqk', q, k).astype(jnp.float32) * (HEAD_DIM ** -0.5)
    s = s + jnp.swapaxes(fq, 1, 2)[:, :, :, None] - jnp.swapaxes(fk, 1, 2)[:, :, None, :]
    visible = k_pos[None, :] <= q_pos[:, None]
    s = jnp.where(visible[None, None], s, NEG_INF)
    w = jax.nn.softmax(s, axis=-1)
    return jnp.einsum('bhqk,bkhd->bqhd', w.astype(v.dtype), v)


def forgetting_attention(q, k, v, logf, pos, past):
    b, t, _ = q.shape
    q = q.reshape(b, t, FA_HEADS, HEAD_DIM)
    k = k.reshape(b, t, FA_HEADS, HEAD_DIM)
    v = v.reshape(b, t, FA_HEADS, HEAD_DIM)
    if past is None:
        f_cum = jnp.cumsum(logf, axis=1)
        nb = t // Q_BLOCK

        def block(args):
            q_blk, f_blk, pos_blk = args
            return fox_attend(q_blk, k, v, f_blk, f_cum, pos_blk, pos)

        q_blocks = jnp.swapaxes(q.reshape(b, nb, Q_BLOCK, FA_HEADS, HEAD_DIM), 0, 1)
        f_blocks = jnp.swapaxes(f_cum.reshape(b, nb, Q_BLOCK, FA_HEADS), 0, 1)
        out = lax.map(block, (q_blocks, f_blocks, pos.reshape(nb, Q_BLOCK)))
        out = jnp.swapaxes(out, 0, 1)
    else:
        past_k, past_v, past_logf = past
        n_past = past_k.shape[1]
        k_all = jnp.concatenate([past_k.astype(k.dtype), k], axis=1)
        v_all = jnp.concatenate([past_v.astype(v.dtype), v], axis=1)
        f_all = jnp.cumsum(jnp.concatenate([past_logf.astype(jnp.float32), logf], axis=1), axis=1)
        k_pos = jnp.arange(n_past + t, dtype=jnp.int32)
        out = fox_attend(q, k_all, v_all, f_all[:, n_past:], f_all, pos, k_pos)
    return out.reshape(b, t, FA_WIDTH), k, v


def spatial_gating(u, v, ln_g, ln_b, w_s, b_s):
    b, t, _ = u.shape
    u = jax.nn.gelu(u)
    v = layer_norm(jax.nn.gelu(v), ln_g, ln_b)
    L = SGU_CHUNK if t >= SGU_CHUNK else t
    nc = t // L
    causal = jnp.tril(jnp.ones((L, L), dtype=bool))
    w_m = jnp.where(causal[None], w_s[:, :L, :L], 0).astype(v.dtype)
    v_c = v.reshape(b, nc, L, SGU_GROUPS, SGU_GROUP_DIM)
    bias = jnp.swapaxes(b_s[:, :L], 0, 1)[None, None, :, :, None].astype(v.dtype)
    mixed = jnp.einsum('gts,bcsgd->bctgd', w_m, v_c) + bias
    return u * mixed.reshape(b, t, SGU_WIDTH), v


def retention_decay():
    return jnp.log1p(-jnp.exp(jnp.linspace(math.log(RET_DECAY_MIN), math.log(RET_DECAY_MAX), RET_HEADS, dtype=jnp.float32)))


def retention_chunk(s, q, k, v, log_g):
    L = q.shape[1]
    n = jnp.arange(L, dtype=jnp.float32)
    diff = n[:, None] - n[None, :]
    causal = diff >= 0
    decay = jnp.where(causal[None], jnp.exp(jnp.where(causal, diff, 0.0)[None] * log_g[:, None, None]), 0.0)
    scores = jnp.einsum('blhd,bmhd->bhlm', q, k) * decay[None]
    inner = jnp.einsum('bhlm,bmhe->blhe', scores, v)
    q_decay = jnp.exp((n[:, None] + 1.0) * log_g[None, :])
    cross = jnp.einsum('blhd,bhde->blhe', q, s) * q_decay[None, :, :, None]
    k_decay = jnp.exp((L - 1.0 - n)[:, None] * log_g[None, :])
    s_new = jnp.exp(L * log_g)[None, :, None, None] * s + jnp.einsum('blhd,blhe->bhde', k * k_decay[None, :, :, None], v)
    return s_new, inner + cross


def multiscale_retention(q, k, v, g, pos, gn_g, s0):
    b, t, _ = q.shape
    log_g = retention_decay()
    q = rotary(q.reshape(b, t, RET_HEADS, HEAD_DIM), pos)
    k = rotary(k.reshape(b, t, RET_HEADS, HEAD_DIM), pos) * (HEAD_DIM ** -0.5)
    v = v.reshape(b, t, RET_HEADS, HEAD_DIM).astype(jnp.float32)
    if s0 is None:
        nc = t // CHUNK

        def to_chunks(a):
            return jnp.swapaxes(a.reshape(b, nc, CHUNK, RET_HEADS, HEAD_DIM), 0, 1)

        def step(s, xs):
            return retention_chunk(s, xs[0], xs[1], xs[2], log_g)

        s_init = jnp.zeros((b, RET_HEADS, HEAD_DIM, HEAD_DIM), jnp.float32)
        s_fin, o = lax.scan(step, s_init, (to_chunks(q), to_chunks(k), to_chunks(v)))
        o = jnp.swapaxes(o, 0, 1).reshape(b, t, RET_HEADS, HEAD_DIM)
    else:
        s_fin, o = retention_chunk(s0.astype(jnp.float32), q, k, v, log_g)
    mu = jnp.mean(o, axis=-1, keepdims=True)
    var = jnp.mean(jnp.square(o - mu), axis=-1, keepdims=True)
    o = ((o - mu) * lax.rsqrt(var + GN_EPS)).reshape(b, t, RET_WIDTH)
    o = o * gn_g.astype(jnp.float32) * jax.nn.silu(g.astype(jnp.float32))
    return o, s_fin


def token_mixers(xn, pos, w_in, b_forget, sgu_ln_g, sgu_ln_b, sgu_w, sgu_b, ret_gn_g, w_out, fa_past, ret_s0):
    dt = xn.dtype
    z = xn @ w_in
    fa_q, fa_k, fa_v, fa_f, sgu_u, sgu_v, ret_q, ret_k, ret_v, ret_g = split_columns(z)
    logf = jax.nn.log_sigmoid(fa_f.astype(jnp.float32) + b_forget.astype(jnp.float32))
    o_a, k, v = forgetting_attention(fa_q, fa_k, fa_v, logf, pos, fa_past)
    o_b, v_sgu = spatial_gating(sgu_u, sgu_v, sgu_ln_g, sgu_ln_b, sgu_w, sgu_b)
    o_c, s_ret = multiscale_retention(ret_q, ret_k, ret_v, ret_g, pos, ret_gn_g, ret_s0)
    o = jnp.concatenate([o_a.astype(dt), o_b.astype(dt), o_c.astype(dt)], axis=-1) @ w_out
    return o, k, v, logf.astype(dt), s_ret.astype(dt), v_sgu


def swiglu(x, w_gate, w_up, w_down):
    return (jax.nn.silu(x @ w_gate) * (x @ w_up)) @ w_down


def moe_swiglu(x, w_router, b_router, w_gate, w_up, w_down):
    logits = (x @ w_router).astype(jnp.float32) + b_router.astype(jnp.float32)
    top_val, top_idx = lax.top_k(logits, TOP_K)
    gates = jax.nn.softmax(top_val, axis=-1)
    combine = jnp.sum(jax.nn.one_hot(top_idx, N_EXPERTS, dtype=jnp.float32) * gates[..., None], axis=-2).astype(x.dtype)
    y = jnp.zeros_like(x)
    for e in range(N_EXPERTS):
        y = y + combine[..., e:e + 1] * swiglu(x, w_gate[e], w_up[e], w_down[e])
    return y


def trunk(x, p, cache_fa_k, cache_fa_v, cache_fa_logf, state_ret, g_mix, w_in, b_forget, sgu_ln_g, sgu_ln_b, sgu_w, sgu_b, ret_gn_g, w_out, g_ffn, w_dense_gate, w_dense_up, w_dense_down, w_router, b_router, w_exp_gate, w_exp_up, w_exp_down, g_ple, w_ple_gate, w_ple_proj, g_final):
    has_past = cache_fa_k is not None
    t = x.shape[1]
    offset = cache_fa_k.shape[2] if has_past else 0
    pos = offset + jnp.arange(t, dtype=jnp.int32)
    h = x
    new_k, new_v, new_logf, new_ret, new_sgu = [], [], [], [], []
    for i in range(DEPTH):
        fa_past = (cache_fa_k[i], cache_fa_v[i], cache_fa_logf[i]) if has_past else None
        ret_s0 = state_ret[i] if has_past else None
        o, k, v, logf, s_ret, v_sgu = token_mixers(rms_norm(h, g_mix[i]), pos, w_in[i], b_forget[i], sgu_ln_g[i], sgu_ln_b[i], sgu_w[i], sgu_b[i], ret_gn_g[i], w_out[i], fa_past, ret_s0)
        h = h + o
        hn = rms_norm(h, g_ffn[i])
        j = i // 2
        if i % 2 == 0:
            h = h + swiglu(hn, w_dense_gate[j], w_dense_up[j], w_dense_down[j])
        else:
            h = h + moe_swiglu(hn, w_router[j], b_router[j], w_exp_gate[j], w_exp_up[j], w_exp_down[j])
        gate = jax.nn.sigmoid(rms_norm(h, g_ple[i]) @ w_ple_gate[i])
        h = h + gate * (p[i] @ w_ple_proj[i])
        new_k.append(k)
        new_v.append(v)
        new_logf.append(logf)
        new_ret.append(s_ret)
        if has_past:
            new_sgu.append(v_sgu)
    y = rms_norm(h, g_final)
    sgu_rows = jnp.stack(new_sgu) if has_past else None
    return y, jnp.stack(new_k), jnp.stack(new_v), jnp.stack(new_logf), jnp.stack(new_ret), sgu_rows


def setup_inputs(seed: int = 0) -> dict:
    key = jax.random.key(seed)
    ks = jax.random.split(key, 32)

    def nrm(k, shape, scale=1.0):
        return jax.random.normal(k, shape, jnp.float32) * scale

    return {
        'x_prompt': nrm(ks[0], (BATCH, SEQ, D_MODEL)),
        'x_sample': nrm(ks[1], (DEC_BATCH, DEC_SEQ, D_MODEL)),
        'cache_fa_k': nrm(ks[2], (DEPTH, DEC_BATCH, PAST_LEN, FA_HEADS, HEAD_DIM)),
        'cache_fa_v': nrm(ks[3], (DEPTH, DEC_BATCH, PAST_LEN, FA_HEADS, HEAD_DIM)),
        'cache_fa_logf': jax.nn.log_sigmoid(FORGET_BIAS_INIT + nrm(ks[4], (DEPTH, DEC_BATCH, PAST_LEN, FA_HEADS))),
        'state_ret': nrm(ks[5], (DEPTH, DEC_BATCH, RET_HEADS, HEAD_DIM, HEAD_DIM), 0.5),
        'p_prompt': nrm(ks[6], (DEPTH, BATCH, SEQ, PLE_DIM)),
        'p_sample': nrm(ks[7], (DEPTH, DEC_BATCH, DEC_SEQ, PLE_DIM)),
        'g_mix': 1.0 + nrm(ks[8], (DEPTH, D_MODEL), 0.02),
        'w_in': nrm(ks[9], (DEPTH, D_MODEL, PROJ_WIDTH), D_MODEL ** -0.5),
        'b_forget': FORGET_BIAS_INIT + nrm(ks[10], (DEPTH, FA_HEADS), 0.1),
        'sgu_ln_g': 1.0 + nrm(ks[11], (DEPTH, SGU_WIDTH), 0.02),
        'sgu_ln_b': nrm(ks[12], (DEPTH, SGU_WIDTH), 0.02),
        'sgu_w': nrm(ks[13], (DEPTH, SGU_GROUPS, SGU_CHUNK, SGU_CHUNK), SGU_CHUNK ** -0.5),
        'sgu_b': 1.0 + nrm(ks[14], (DEPTH, SGU_GROUPS, SGU_CHUNK), 0.02),
        'ret_gn_g': 1.0 + nrm(ks[15], (DEPTH, RET_WIDTH), 0.02),
        'w_out': nrm(ks[16], (DEPTH, MIX_WIDTH, D_MODEL), MIX_WIDTH ** -0.5),
        'g_ffn': 1.0 + nrm(ks[17], (DEPTH, D_MODEL), 0.02),
        'w_dense_gate': nrm(ks[18], (N_DENSE, D_MODEL, DENSE_FF), D_MODEL ** -0.5),
        'w_dense_up': nrm(ks[19], (N_DENSE, D_MODEL, DENSE_FF), D_MODEL ** -0.5),
        'w_dense_down': nrm(ks[20], (N_DENSE, DENSE_FF, D_MODEL), DENSE_FF ** -0.5),
        'w_router': nrm(ks[21], (N_MOE, D_MODEL, N_EXPERTS), D_MODEL ** -0.5),
        'b_router': nrm(ks[22], (N_MOE, N_EXPERTS), 0.01),
        'w_exp_gate': nrm(ks[23], (N_MOE, N_EXPERTS, D_MODEL, EXPERT_FF), D_MODEL ** -0.5),
        'w_exp_up': nrm(ks[24], (N_MOE, N_EXPERTS, D_MODEL, EXPERT_FF), D_MODEL ** -0.5),
        'w_exp_down': nrm(ks[25], (N_MOE, N_EXPERTS, EXPERT_FF, D_MODEL), EXPERT_FF ** -0.5),
        'g_ple': 1.0 + nrm(ks[26], (DEPTH, D_MODEL), 0.02),
        'w_ple_gate': nrm(ks[27], (DEPTH, D_MODEL, D_MODEL), D_MODEL ** -0.5),
        'w_ple_proj': nrm(ks[28], (DEPTH, PLE_DIM, D_MODEL), PLE_DIM ** -0.5),
        'g_final': 1.0 + nrm(ks[29], (D_MODEL,), 0.02),
    }


def reference(x_prompt, x_sample, cache_fa_k, cache_fa_v, cache_fa_logf, state_ret, p_prompt, p_sample, g_mix, w_in, b_forget, sgu_ln_g, sgu_ln_b, sgu_w, sgu_b, ret_gn_g, w_out, g_ffn, w_dense_gate, w_dense_up, w_dense_down, w_router, b_router, w_exp_gate, w_exp_up, w_exp_down, g_ple, w_ple_gate, w_ple_proj, g_final):
    weights = (g_mix, w_in, b_forget, sgu_ln_g, sgu_ln_b, sgu_w, sgu_b, ret_gn_g, w_out, g_ffn, w_dense_gate, w_dense_up, w_dense_down, w_router, b_router, w_exp_gate, w_exp_up, w_exp_down, g_ple, w_ple_gate, w_ple_proj, g_final)
    y_prompt, fa_k_prompt, fa_v_prompt, fa_logf_prompt, state_ret_prompt, _ = trunk(x_prompt, p_prompt, None, None, None, None, *weights)
    y_sample, fa_k_sample, fa_v_sample, fa_logf_sample, state_ret_sample, sgu_v_sample = trunk(x_sample, p_sample, cache_fa_k, cache_fa_v, cache_fa_logf, state_ret, *weights)
    return (y_prompt, y_sample, fa_k_prompt, fa_v_prompt, fa_logf_prompt, state_ret_prompt, fa_k_sample, fa_v_sample, fa_logf_sample, state_ret_sample, sgu_v_sample)
```

```python
import functools
import math

import jax
import jax.numpy as jnp
from jax import lax
from jax.experimental import pallas as pl
from jax.experimental.pallas import tpu as pltpu

F32 = jnp.float32
BF16 = jnp.bfloat16

HEAD_DIM = 64
LANES = 128
ROPE_BASE = 10000.0
RET_DECAY_MIN = 1.0 / 32.0
RET_DECAY_MAX = 1.0 / 512.0
NORM_EPS = 1e-6
GN_EPS = 1e-5
NEG_INF = -1e30
TOP_K = 2

TOKEN_TILE = 512
ATTN_TILE = 512
RET_CHUNK = 256
MOE_ROW_BLOCK = 1024
VMEM_LIMIT = 48 * 1024 * 1024


def _cparams(sem, vmem=VMEM_LIMIT):
    return pltpu.CompilerParams(dimension_semantics=sem, vmem_limit_bytes=vmem)


def _full(shape):
    zeros = (0,) * len(shape)
    return pl.BlockSpec(shape, lambda *_: zeros)


def _resident(shape):
    zeros = (0,) * len(shape)
    return pl.BlockSpec(shape, lambda *_: zeros, pipeline_mode=pl.Buffered(1))


def _rms(x, g):
    return x * lax.rsqrt(jnp.mean(x * x, axis=-1, keepdims=True) + NORM_EPS) * g


def _silu(x):
    return x * (1.0 / (1.0 + jnp.exp(-x)))


def _split3(x):
    p1 = x.astype(BF16)
    r1 = x - p1.astype(F32)
    p2 = r1.astype(BF16)
    p3 = (r1 - p2.astype(F32)).astype(BF16)
    return p1, p2, p3


def _nt_dot(a, b):
    return lax.dot_general(a, b, (((1,), (1,)), ((), ())), preferred_element_type=F32)


def _inproj_kernel(*refs, offs, tiles_per_seq, do_cumsum, tm):
    (h_ref, g_ref, w_ref, bf_ref, cos_ref, sin_ref, lng_ref, lnb_ref) = refs[:8]
    outs = refs[8:]
    if do_cumsum:
        (q_ref, k32_ref, v32_ref, kb_ref, vb_ref, lf_ref, fc_ref, u_ref, sv_ref,
         rq_ref, rk_ref, rv_ref, rg_ref, carry_ref) = outs
    else:
        (q_ref, k32_ref, v32_ref, kb_ref, vb_ref, lf_ref, u_ref, sv_ref,
         rq_ref, rk_ref, rv_ref, rg_ref) = outs

    xn = _rms(h_ref[...], g_ref[...]).astype(BF16)

    def proj(name):
        a, b = offs[name]
        return jnp.dot(xn, w_ref[:, a:b], preferred_element_type=F32)

    q = proj("q")
    q_ref[...] = (q * (HEAD_DIM ** -0.5)).astype(BF16)
    k = proj("k")
    k32_ref[...] = k
    kb_ref[...] = k.astype(BF16)
    v = proj("v")
    v32_ref[...] = v
    vb_ref[...] = v.astype(BF16)

    u_ref[...] = jax.nn.gelu(proj("u"))
    sv = jax.nn.gelu(proj("sv"))
    mu = jnp.mean(sv, axis=-1, keepdims=True)
    var = jnp.mean(jnp.square(sv - mu), axis=-1, keepdims=True)
    sv_ref[...] = (sv - mu) * lax.rsqrt(var + GN_EPS) * lng_ref[...] + lnb_ref[...]

    cos = cos_ref[...]
    sin = sin_ref[...]
    rq_ref[...] = (proj("rq") * cos + proj("rqr") * sin).astype(BF16)
    rk_ref[...] = ((proj("rk") * cos + proj("rkr") * sin) * (HEAD_DIM ** -0.5)).astype(BF16)
    rv_ref[...] = proj("rv").astype(BF16)
    rg_ref[...] = _silu(proj("rg")).astype(BF16)

    f = proj("f") + bf_ref[...]
    lf = jnp.minimum(f, 0.0) - jnp.log1p(jnp.exp(-jnp.abs(f)))
    nh = lf_ref.shape[-1]
    lf_ref[...] = lf[:, :nh]

    if do_cumsum:
        i = pl.program_id(0)

        @pl.when(i % tiles_per_seq == 0)
        def _():
            carry_ref[...] = jnp.zeros_like(carry_ref)

        row = lax.broadcasted_iota(jnp.int32, (tm, tm), 0)
        col = lax.broadcasted_iota(jnp.int32, (tm, tm), 1)
        tril = jnp.where(col <= row, 1.0, 0.0).astype(BF16)
        pieces = jnp.concatenate(_split3(lf), axis=1)
        cs = jnp.dot(tril, pieces, preferred_element_type=F32)
        fc = cs[:, :LANES] + cs[:, LANES:2 * LANES] + cs[:, 2 * LANES:] + carry_ref[...]
        carry_ref[...] = fc[tm - 1:tm, :]
        fc_ref[...] = fc[:, :nh]


def _inproj(h, g, w, bf, cos, sin, lng, lnb, *, offs, dims, seq_len, do_cumsum):
    n, d = h.shape
    tm = TOKEN_TILE if n % TOKEN_TILE == 0 else n
    nt = n // tm
    tiles_per_seq = max(seq_len // tm, 1)
    tab_blocks = cos.shape[0] // tm
    faw, sw, rw, nh = dims["faw"], dims["sw"], dims["rw"], dims["heads"]
    tile = lambda width: pl.BlockSpec((tm, width), lambda i: (i, 0))
    tab = pl.BlockSpec((tm, rw), lambda i: (i % tab_blocks, 0))
    in_specs = [tile(d), _full((1, d)), _resident(w.shape), _full((1, LANES)), tab, tab,
                _full((1, sw)), _full((1, sw))]
    out_shape = [jax.ShapeDtypeStruct((n, faw), BF16), jax.ShapeDtypeStruct((n, faw), F32),
                 jax.ShapeDtypeStruct((n, faw), F32), jax.ShapeDtypeStruct((n, faw), BF16),
                 jax.ShapeDtypeStruct((n, faw), BF16), jax.ShapeDtypeStruct((n, nh), F32)]
    out_specs = [tile(faw)] * 5 + [tile(nh)]
    if do_cumsum:
        out_shape.append(jax.ShapeDtypeStruct((n, nh), F32))
        out_specs.append(tile(nh))
    out_shape += [jax.ShapeDtypeStruct((n, sw), F32), jax.ShapeDtypeStruct((n, sw), F32)]
    out_specs += [tile(sw), tile(sw)]
    out_shape += [jax.ShapeDtypeStruct((n, rw), BF16)] * 4
    out_specs += [tile(rw)] * 4
    scratch = [pltpu.VMEM((1, LANES), F32)] if do_cumsum else []
    return pl.pallas_call(
        functools.partial(_inproj_kernel, offs=offs, tiles_per_seq=tiles_per_seq,
                          do_cumsum=do_cumsum, tm=tm),
        grid=(nt,), in_specs=in_specs, out_specs=out_specs, out_shape=out_shape,
        scratch_shapes=scratch, compiler_params=_cparams(("arbitrary",)),
        name="inproj",
    )(h, g, w, bf, cos, sin, lng, lnb)


def _fox_prompt_kernel(q_ref, k_ref, v_ref, fq_ref, fk_ref, o_ref, m_ref, l_ref, acc_ref, *, t):
    p = pl.program_id(1)
    qi = pl.program_id(2)
    q2 = q_ref[...]
    lane = lax.broadcasted_iota(jnp.int32, (1, LANES), 1)
    lo = lane < HEAD_DIM
    zero = jnp.zeros_like(q2)
    qh = (jnp.where(lo, q2, zero), jnp.where(lo, zero, q2))
    fq = fq_ref[0]

    m_ref[...] = jnp.full(m_ref.shape, NEG_INF, F32)
    l_ref[...] = jnp.zeros(l_ref.shape, F32)
    acc_ref[...] = jnp.zeros(acc_ref.shape, F32)

    def step(ki, masked):
        start = pl.multiple_of(ki * t, t)
        k2 = k_ref[pl.ds(start, t), :]
        v2 = v_ref[pl.ds(start, t), :]
        fk = fk_ref[0, ki, p]
        for hh in range(2):
            s = _nt_dot(qh[hh], k2) + fq[:, hh:hh + 1] - fk[hh:hh + 1, :]
            if masked:
                row = lax.broadcasted_iota(jnp.int32, (t, t), 0)
                col = lax.broadcasted_iota(jnp.int32, (t, t), 1)
                s = jnp.where(col <= row, s, NEG_INF)
            m_prev = m_ref[hh]
            m_new = jnp.maximum(m_prev, jnp.max(s, axis=1, keepdims=True))
            alpha = jnp.exp(m_prev - m_new)
            pr = jnp.exp(s - m_new)
            l_ref[hh] = alpha * l_ref[hh] + jnp.sum(pr, axis=1, keepdims=True)
            acc_ref[hh] = alpha * acc_ref[hh] + jnp.dot(pr.astype(BF16), v2,
                                                        preferred_element_type=F32)
            m_ref[hh] = m_new

    def body(ki, carry):
        step(ki, False)
        return carry

    lax.fori_loop(0, qi, body, 0)
    step(qi, True)
    o_ref[...] = jnp.where(lo, acc_ref[0] / l_ref[0], acc_ref[1] / l_ref[1]).astype(BF16)


def _fox_prompt(qb, kb, vb, fc, *, batch, seq, heads):
    n, faw = qb.shape
    t = ATTN_TILE if seq % ATTN_TILE == 0 else seq
    nq = seq // t
    npairs = heads // 2
    fq = fc.reshape(n, npairs, 2).transpose(1, 0, 2)
    fk = fc.reshape(batch, nq, t, npairs, 2).transpose(0, 1, 3, 4, 2)
    return pl.pallas_call(
        functools.partial(_fox_prompt_kernel, t=t),
        grid=(batch, npairs, nq),
        in_specs=[pl.BlockSpec((t, LANES), lambda b, p, i: (b * nq + i, p)),
                  pl.BlockSpec((seq, LANES), lambda b, p, i: (b, p)),
                  pl.BlockSpec((seq, LANES), lambda b, p, i: (b, p)),
                  pl.BlockSpec((1, t, 2), lambda b, p, i: (p, b * nq + i, 0)),
                  pl.BlockSpec((1, nq, npairs, 2, t), lambda b, p, i: (b, 0, 0, 0, 0))],
        out_specs=pl.BlockSpec((t, LANES), lambda b, p, i: (b * nq + i, p)),
        out_shape=jax.ShapeDtypeStruct((n, faw), BF16),
        scratch_shapes=[pltpu.VMEM((2, t, 1), F32), pltpu.VMEM((2, t, 1), F32),
                        pltpu.VMEM((2, t, LANES), F32)],
        compiler_params=_cparams(("arbitrary", "arbitrary", "arbitrary")),
        name="fox_prompt",
    )(qb, kb, vb, fq, fk)


def _fox_sample_kernel(q_ref, kn_ref, vn_ref, pk_ref, pv_ref, plft_ref, lf_ref, lft_ref, o_ref,
                       *, heads, ts, past):
    hi = lax.Precision.HIGHEST
    r_ = lax.broadcasted_iota(jnp.int32, (past, past), 0)
    c_ = lax.broadcasted_iota(jnp.int32, (past, past), 1)
    after = jnp.where(r_ > c_, 1.0, 0.0).astype(F32)
    g_row = jnp.dot(plft_ref[0], after, precision=hi, preferred_element_type=F32)
    rr = lax.broadcasted_iota(jnp.int32, (LANES, LANES), 0)
    cc = lax.broadcasted_iota(jnp.int32, (LANES, LANES), 1)
    tril = jnp.where(cc <= rr, 1.0, 0.0).astype(F32)
    triu = jnp.where(rr <= cc, 1.0, 0.0).astype(F32)
    c_col = jnp.dot(tril, lf_ref[0], precision=hi, preferred_element_type=F32)[:ts]
    c_row = jnp.dot(lft_ref[0], triu, precision=hi, preferred_element_type=F32)
    visible = (lax.broadcasted_iota(jnp.int32, (ts, LANES), 1)
               <= lax.broadcasted_iota(jnp.int32, (ts, LANES), 0))
    lane = lax.broadcasted_iota(jnp.int32, (1, LANES), 1)
    lo = lane < HEAD_DIM
    for p in range(heads // 2):
        sl = slice(p * LANES, (p + 1) * LANES)
        q2 = q_ref[:, sl]
        kp2 = pk_ref[0, :, sl].astype(BF16)
        vp2 = pv_ref[0, :, sl].astype(BF16)
        kn2 = kn_ref[:, sl]
        vn2 = vn_ref[:, sl]
        zero = jnp.zeros_like(q2)
        res = []
        for hh in range(2):
            h = 2 * p + hh
            qh = jnp.where(lo, q2, zero) if hh == 0 else jnp.where(lo, zero, q2)
            s_p = _nt_dot(qh, kp2) + c_col[:, h:h + 1] + g_row[h:h + 1, :]
            s_n = _nt_dot(qh, kn2) + c_col[:, h:h + 1] - c_row[h:h + 1, :]
            s_n = jnp.where(visible, s_n, NEG_INF)
            m = jnp.maximum(jnp.max(s_p, axis=1, keepdims=True), jnp.max(s_n, axis=1, keepdims=True))
            pp = jnp.exp(s_p - m)
            pn = jnp.exp(s_n - m)
            l = jnp.sum(pp, axis=1, keepdims=True) + jnp.sum(pn, axis=1, keepdims=True)
            o = (jnp.dot(pp.astype(BF16), vp2, preferred_element_type=F32)
                 + jnp.dot(pn.astype(BF16), vn2, preferred_element_type=F32))
            res.append(o / l)
        o_ref[:, sl] = jnp.where(lo, res[0], res[1]).astype(BF16)


def _fox_sample(qb, kb, vb, lf, past_k, past_v, past_lf, *, batch, ts, heads):
    n, faw = qb.shape
    past = past_k.shape[1]
    plft = past_lf.transpose(0, 2, 1)
    lf3 = lf.reshape(batch, ts, heads)
    lfp = jnp.pad(lf3, ((0, 0), (0, LANES - ts), (0, LANES - heads)))
    lftp = jnp.pad(lf3.transpose(0, 2, 1), ((0, 0), (0, 0), (0, LANES - ts)))
    padr = lambda a: jnp.pad(a.reshape(batch, ts, faw), ((0, 0), (0, LANES - ts), (0, 0))
                             ).reshape(batch * LANES, faw)
    row = lambda rows: pl.BlockSpec((rows, faw), lambda b: (b, 0))
    return pl.pallas_call(
        functools.partial(_fox_sample_kernel, heads=heads, ts=ts, past=past),
        grid=(batch,),
        in_specs=[row(ts), row(LANES), row(LANES),
                  pl.BlockSpec((1, past, faw), lambda b: (b, 0, 0)),
                  pl.BlockSpec((1, past, faw), lambda b: (b, 0, 0)),
                  pl.BlockSpec((1, heads, past), lambda b: (b, 0, 0)),
                  pl.BlockSpec((1, LANES, LANES), lambda b: (b, 0, 0)),
                  pl.BlockSpec((1, heads, LANES), lambda b: (b, 0, 0))],
        out_specs=row(ts),
        out_shape=jax.ShapeDtypeStruct((n, faw), BF16),
        compiler_params=_cparams(("arbitrary",)),
        name="fox_sample",
    )(qb, padr(kb), padr(vb), past_k, past_v, plft, lfp, lftp)


def _sgu_kernel(u_ref, v_ref, w_ref, b_ref, o_ref, *, chunk, chunks_per_tile, groups):
    width = u_ref.shape[-1]
    row = lax.broadcasted_iota(jnp.int32, (chunk, chunk), 0)
    col = lax.broadcasted_iota(jnp.int32, (chunk, chunk), 1)
    causal = col <= row
    grp = lax.broadcasted_iota(jnp.int32, (1, width), 1) // HEAD_DIM
    wm = [jnp.where(causal, w_ref[g], 0.0).astype(BF16) for g in range(groups)]
    bias = b_ref[...]
    for c in range(chunks_per_tile):
        sl = slice(c * chunk, (c + 1) * chunk)
        vc = v_ref[sl, :].astype(BF16)
        mixed = jnp.zeros((chunk, width), F32)
        for g in range(groups):
            mixed = jnp.where(grp == g, jnp.dot(wm[g], vc, preferred_element_type=F32), mixed)
        o_ref[sl, :] = (u_ref[sl, :] * (mixed + bias)).astype(BF16)


def _sgu(u, v, w, bias):
    n, width = u.shape
    groups, chunk, _ = w.shape
    tile = TOKEN_TILE if (n % TOKEN_TILE == 0 and TOKEN_TILE % chunk == 0) else chunk
    tok = pl.BlockSpec((tile, width), lambda i: (i, 0))
    return pl.pallas_call(
        functools.partial(_sgu_kernel, chunk=chunk, chunks_per_tile=tile // chunk, groups=groups),
        grid=(n // tile,),
        in_specs=[tok, tok, _full(w.shape), _full(bias.shape)],
        out_specs=tok,
        out_shape=jax.ShapeDtypeStruct((n, width), BF16),
        compiler_params=_cparams(("arbitrary",)),
        name="sgu",
    )(u, v, w, bias)


def _retention_kernel(q_ref, k_ref, v_ref, g_ref, s0_ref, dmat_ref, qdec_ref, kdec_ref, sdec_ref,
                      bmask_ref, avg_ref, gn_ref, o_ref, sfin_ref, s_ref, *, npairs):
    c = pl.program_id(1)

    @pl.when(c == 0)
    def _():
        s_ref[...] = s0_ref[0]

    lane = lax.broadcasted_iota(jnp.int32, (1, LANES), 1)
    lo = lane < HEAD_DIM
    avg = avg_ref[...]

    def group_mean(x):
        x1 = x.astype(BF16)
        x2 = (x - x1.astype(F32)).astype(BF16)
        return (jnp.dot(x1, avg, preferred_element_type=F32)
                + jnp.dot(x2, avg, preferred_element_type=F32))

    for p in range(npairs):
        sl = slice(p * LANES, (p + 1) * LANES)
        q2 = q_ref[:, sl]
        k2 = k_ref[:, sl]
        v2 = v_ref[:, sl]
        zero = jnp.zeros_like(q2)
        inner = []
        for hh in range(2):
            qh = jnp.where(lo, q2, zero) if hh == 0 else jnp.where(lo, zero, q2)
            sc = _nt_dot(qh, k2) * dmat_ref[2 * p + hh]
            inner.append(jnp.dot(sc.astype(BF16), v2, preferred_element_type=F32))
        state = s_ref[p]
        cross = jnp.dot(q2, state.astype(BF16), preferred_element_type=F32) * qdec_ref[:, sl]
        o = jnp.where(lo, inner[0], inner[1]) + cross
        kd_t = (k2.astype(F32) * kdec_ref[:, sl]).T.astype(BF16)
        s_ref[p] = state * sdec_ref[p] + jnp.dot(kd_t, v2, preferred_element_type=F32) * bmask_ref[...]
        mu = group_mean(o)
        d = o - mu
        var = group_mean(d * d)
        on = d * lax.rsqrt(var + GN_EPS)
        o_ref[:, sl] = (on * gn_ref[:, sl] * g_ref[:, sl].astype(F32)).astype(BF16)

    sfin_ref[0] = s_ref[...]


def _retention(rq, rk, rv, rg, s0, tabs, gn, *, batch, rows_per_seq, blk):
    n, rw = rq.shape
    npairs = rw // LANES
    nc = rows_per_seq // blk
    tok = pl.BlockSpec((blk, rw), lambda b, c: (b * nc + c, 0))
    st = pl.BlockSpec((1, npairs, LANES, LANES), lambda b, c: (b, 0, 0, 0))
    dmat, qdec, kdec, sdec, bmask, avg = tabs
    return pl.pallas_call(
        functools.partial(_retention_kernel, npairs=npairs),
        grid=(batch, nc),
        in_specs=[tok, tok, tok, tok, st, _full(dmat.shape), _full(qdec.shape), _full(kdec.shape),
                  _full(sdec.shape), _full(bmask.shape), _full(avg.shape), _full(gn.shape)],
        out_specs=[tok, st],
        out_shape=[jax.ShapeDtypeStruct((n, rw), BF16),
                   jax.ShapeDtypeStruct((batch, npairs, LANES, LANES), F32)],
        scratch_shapes=[pltpu.VMEM((npairs, LANES, LANES), F32)],
        compiler_params=_cparams(("arbitrary", "arbitrary")),
        name="retention",
    )(rq, rk, rv, rg, s0, dmat, qdec, kdec, sdec, bmask, avg, gn)


def _retention_tables(ret_heads, true_len, blk):
    log_g = jnp.log1p(-jnp.exp(jnp.linspace(math.log(RET_DECAY_MIN), math.log(RET_DECAY_MAX),
                                            ret_heads, dtype=F32)))
    n = jnp.arange(blk, dtype=F32)
    diff = n[:, None] - n[None, :]
    causal = diff >= 0
    dmat = jnp.where(causal[None], jnp.exp(jnp.where(causal, diff, 0.0)[None] * log_g[:, None, None]), 0.0)
    per_lane = jnp.repeat(log_g, HEAD_DIM)
    qdec = jnp.exp((n[:, None] + 1.0) * per_lane[None, :])
    kdec = jnp.exp((true_len - 1.0 - n)[:, None] * per_lane[None, :])
    npairs = ret_heads // 2
    lane_head = jnp.arange(LANES) // HEAD_DIM
    bmask = (lane_head[:, None] == lane_head[None, :]).astype(F32)
    sdec = jnp.exp(true_len * per_lane).reshape(npairs, LANES)[:, :, None] * bmask[None]
    avg = (bmask / HEAD_DIM).astype(BF16)
    return dmat, qdec, kdec, sdec, bmask, avg


def _outproj_kernel(*refs, with_router, n_experts, faw, sw):
    if with_router:
        (h_ref, oa_ref, ob_ref, oc_ref, w_ref, g_ref, wr_ref, br_ref, hnew_ref, hn_ref, eg_ref) = refs
    else:
        (h_ref, oa_ref, ob_ref, oc_ref, w_ref, g_ref, hnew_ref, hn_ref) = refs
    o = (jnp.dot(oa_ref[...], w_ref[:faw, :], preferred_element_type=F32)
         + jnp.dot(ob_ref[...], w_ref[faw:faw + sw, :], preferred_element_type=F32)
         + jnp.dot(oc_ref[...], w_ref[faw + sw:, :], preferred_element_type=F32))
    h = h_ref[...] + o
    hnew_ref[...] = h
    hn = _rms(h, g_ref[...])
    hn_ref[...] = hn.astype(BF16)
    if with_router:
        x1 = hn.astype(BF16)
        x2 = (hn - x1.astype(F32)).astype(BF16)
        a = jnp.dot(x1, wr_ref[...], preferred_element_type=F32)
        b = jnp.dot(x2, wr_ref[:, :LANES], preferred_element_type=F32)
        logits = a[:, :LANES] + a[:, LANES:] + b + br_ref[...]
        lane = lax.broadcasted_iota(jnp.int32, logits.shape, 1)
        lanef = lane.astype(F32)
        logits = jnp.where(lane < n_experts, logits, -jnp.inf)
        m1 = jnp.max(logits, axis=1, keepdims=True)
        i1 = jnp.min(jnp.where(logits == m1, lanef, float(LANES)), axis=1, keepdims=True)
        rest = jnp.where(lanef == i1, -jnp.inf, logits)
        m2 = jnp.max(rest, axis=1, keepdims=True)
        i2 = jnp.min(jnp.where(rest == m2, lanef, float(LANES)), axis=1, keepdims=True)
        e2 = jnp.exp(m2 - m1)
        g1 = 1.0 / (1.0 + e2)
        g2 = e2 / (1.0 + e2)
        eg_ref[...] = jnp.where(lane == 0, g1, jnp.where(lane == 1, g2,
                                jnp.where(lane == 2, i1, jnp.where(lane == 3, i2, 0.0))))


def _outproj(h, oa, ob, oc, w, g, router=None):
    n, d = h.shape
    tm = TOKEN_TILE if n % TOKEN_TILE == 0 else n
    faw, sw, rw = oa.shape[1], ob.shape[1], oc.shape[1]
    tile = lambda width: pl.BlockSpec((tm, width), lambda i: (i, 0))
    in_specs = [tile(d), tile(faw), tile(sw), tile(rw), _resident(w.shape), _full((1, d))]
    args = [h, oa, ob, oc, w, g]
    out_shape = [jax.ShapeDtypeStruct((n, d), F32), jax.ShapeDtypeStruct((n, d), BF16)]
    out_specs = [tile(d), tile(d)]
    n_experts = 0
    if router is not None:
        wr, br, n_experts = router
        in_specs += [_full(wr.shape), _full(br.shape)]
        args += [wr, br]
        out_shape.append(jax.ShapeDtypeStruct((n, LANES), F32))
        out_specs.append(tile(LANES))
    return pl.pallas_call(
        functools.partial(_outproj_kernel, with_router=router is not None, n_experts=n_experts,
                          faw=faw, sw=sw),
        grid=(n // tm,), in_specs=in_specs, out_specs=out_specs, out_shape=out_shape,
        compiler_params=_cparams(("arbitrary",)),
        name="outproj",
    )(*args)


def _dense_ffn_kernel(h_ref, x_ref, wg_ref, wu_ref, wd_ref, o_ref, acc_ref, *, fc, nchunks):
    x = x_ref[...]
    for c in range(nchunks):
        sl = slice(c * fc, (c + 1) * fc)
        g = jnp.dot(x, wg_ref[:, sl], preferred_element_type=F32)
        u = jnp.dot(x, wu_ref[:, sl], preferred_element_type=F32)
        a = (_silu(g) * u).astype(BF16)
        y = jnp.dot(a, wd_ref[sl, :], preferred_element_type=F32)
        if c == 0:
            acc_ref[...] = y
        else:
            acc_ref[...] += y
    o_ref[...] = h_ref[...] + acc_ref[...]


def _dense_ffn(h, hn, wg, wu, wd):
    n, d = h.shape
    ff = wg.shape[1]
    tm = TOKEN_TILE if n % TOKEN_TILE == 0 else n
    fc = 256 if ff % 256 == 0 else ff
    tile = pl.BlockSpec((tm, d), lambda i: (i, 0))
    return pl.pallas_call(
        functools.partial(_dense_ffn_kernel, fc=fc, nchunks=ff // fc),
        grid=(n // tm,),
        in_specs=[tile, tile, _resident(wg.shape), _resident(wu.shape), _resident(wd.shape)],
        out_specs=tile,
        out_shape=jax.ShapeDtypeStruct((n, d), F32),
        scratch_shapes=[pltpu.VMEM((tm, d), F32)],
        compiler_params=_cparams(("arbitrary",)),
        name="dense_ffn",
    )(h, hn, wg, wu, wd)


def _moe_ffn_kernel(be_ref, nu_ref, x_ref, gate_ref, wg_ref, wu_ref, wd_ref, o_ref, acc_ref, *, nch):
    b = pl.program_id(0)
    c = pl.program_id(1)

    @pl.when(b < nu_ref[0])
    def _():
        x = x_ref[...]
        g = jnp.dot(x, wg_ref[0], preferred_element_type=F32)
        u = jnp.dot(x, wu_ref[0], preferred_element_type=F32)
        a = (_silu(g) * u).astype(BF16)
        y = jnp.dot(a, wd_ref[0], preferred_element_type=F32)

        @pl.when(c == 0)
        def _():
            acc_ref[...] = y

        @pl.when(c > 0)
        def _():
            acc_ref[...] += y

        @pl.when(c == nch - 1)
        def _():
            o_ref[...] = (acc_ref[...] * gate_ref[...]).astype(BF16)


def _moe_ffn(xs, row_gate, block_expert, n_used, wg, wu, wd, *, rb):
    r, d = xs.shape
    ff = wg.shape[2]
    nblk = r // rb
    nch = 4 if ff % (4 * LANES) == 0 else 1
    fc = ff // nch

    def live(b, nu):
        return jnp.minimum(b, nu[0] - 1)

    def chunk(b, c, nu):
        return jnp.where(b < nu[0], c, nch - 1)

    grid_spec = pltpu.PrefetchScalarGridSpec(
        num_scalar_prefetch=2,
        grid=(nblk, nch),
        in_specs=[pl.BlockSpec((rb, d), lambda b, c, be, nu: (live(b, nu), 0)),
                  pl.BlockSpec((rb, 1), lambda b, c, be, nu: (live(b, nu), 0)),
                  pl.BlockSpec((1, d, fc), lambda b, c, be, nu: (be[live(b, nu)], 0, chunk(b, c, nu))),
                  pl.BlockSpec((1, d, fc), lambda b, c, be, nu: (be[live(b, nu)], 0, chunk(b, c, nu))),
                  pl.BlockSpec((1, fc, d), lambda b, c, be, nu: (be[live(b, nu)], chunk(b, c, nu), 0))],
        out_specs=pl.BlockSpec((rb, d), lambda b, c, be, nu: (live(b, nu), 0)),
        scratch_shapes=[pltpu.VMEM((rb, d), F32)],
    )
    return pl.pallas_call(
        functools.partial(_moe_ffn_kernel, nch=nch),
        grid_spec=grid_spec,
        out_shape=jax.ShapeDtypeStruct((r, d), BF16),
        compiler_params=_cparams(("arbitrary", "arbitrary")),
        name="moe_ffn",
    )(block_expert, n_used, xs, row_gate, wg, wu, wd)


def _gather_rows(x, idx):
    return jnp.take(x, idx, axis=0)


def _moe(hn, eg, wg, wu, wd, *, rb):
    n, d = hn.shape
    n_experts = wg.shape[0]
    gates = eg[:, :TOP_K]
    experts = eg[:, TOP_K:2 * TOP_K].astype(jnp.int32)
    flat_e = experts.reshape(-1)
    onehot = (flat_e[:, None] == jnp.arange(n_experts, dtype=jnp.int32)[None, :]).astype(jnp.int32)
    rank = jnp.cumsum(onehot, axis=0) - onehot
    count = jnp.sum(onehot, axis=0)
    padded = ((count + rb - 1) // rb) * rb
    ends = jnp.cumsum(padded)
    offset = ends - padded
    pos = offset[flat_e] + jnp.take_along_axis(rank, flat_e[:, None], axis=1)[:, 0]
    r_tot = TOP_K * n + n_experts * rb
    nblk = r_tot // rb
    src_tok = jnp.zeros((r_tot,), jnp.int32).at[pos].set(jnp.arange(TOP_K * n, dtype=jnp.int32) // TOP_K)
    row_gate = jnp.zeros((r_tot,), F32).at[pos].set(gates.reshape(-1))
    block_start = jnp.arange(nblk, dtype=jnp.int32) * rb
    block_expert = jnp.minimum(jnp.sum((block_start[:, None] >= ends[None, :]).astype(jnp.int32), axis=1),
                               n_experts - 1).astype(jnp.int32)
    n_used = (ends[-1:] // rb).astype(jnp.int32)
    xs = _gather_rows(hn, src_tok)
    ys = _moe_ffn(xs, row_gate[:, None], block_expert, n_used, wg, wu, wd, rb=rb)
    return _gather_rows(ys, pos).reshape(n, TOP_K * d)


def _ple_kernel(*refs, with_moe, final, d):
    refs = list(refs)
    h_ref = refs.pop(0)
    y_ref = refs.pop(0) if with_moe else None
    p_ref, g_ref, wg_ref, wp_ref = refs[:4]
    refs = refs[4:]
    gf_ref = refs.pop(0) if final else None
    o_ref = refs[0]
    h = h_ref[...]
    if with_moe:
        h = h + y_ref[:, :d].astype(F32) + y_ref[:, d:].astype(F32)
    gate = jax.nn.sigmoid(jnp.dot(_rms(h, g_ref[...]).astype(BF16), wg_ref[...],
                                  preferred_element_type=F32))
    h = h + gate * jnp.dot(p_ref[...].astype(BF16), wp_ref[...], preferred_element_type=F32)
    if final:
        h = _rms(h, gf_ref[...])
    o_ref[...] = h


def _ple(h, y, p, g, wg, wp, g_final=None):
    n, d = h.shape
    tm = TOKEN_TILE if n % TOKEN_TILE == 0 else n
    tile = lambda width: pl.BlockSpec((tm, width), lambda i: (i, 0))
    in_specs, args = [tile(d)], [h]
    if y is not None:
        in_specs.append(tile(y.shape[1]))
        args.append(y)
    in_specs += [tile(p.shape[1]), _full((1, d)), _resident(wg.shape), _resident(wp.shape)]
    args += [p, g, wg, wp]
    if g_final is not None:
        in_specs.append(_full((1, d)))
        args.append(g_final)
    return pl.pallas_call(
        functools.partial(_ple_kernel, with_moe=y is not None, final=g_final is not None, d=d),
        grid=(n // tm,), in_specs=in_specs, out_specs=tile(d),
        out_shape=jax.ShapeDtypeStruct((n, d), F32),
        compiler_params=_cparams(("arbitrary",)),
        name="ple",
    )(*args)


def _prep_w_in(w_in, dims):
    faw, sw, rw, nh = dims["faw"], dims["sw"], dims["rw"], dims["heads"]
    sizes = (faw, faw, faw, nh, sw, sw, rw, rw, rw, rw)
    parts, o = [], 0
    for s in sizes:
        parts.append(w_in[..., o:o + s])
        o += s
    fa_q, fa_k, fa_v, fa_f, sgu_u, sgu_v, ret_q, ret_k, ret_v, ret_g = parts

    def rot_cols(w):
        lead = w.shape[:-1]
        wh = w.reshape(lead + (rw // HEAD_DIM, 2, HEAD_DIM // 2))
        return jnp.stack([-wh[..., 1, :], wh[..., 0, :]], axis=-2).reshape(lead + (rw,))

    f_pad = jnp.pad(fa_f, [(0, 0)] * (w_in.ndim - 1) + [(0, LANES - nh)])
    cols = [("q", fa_q), ("k", fa_k), ("v", fa_v), ("u", sgu_u), ("sv", sgu_v), ("rq", ret_q),
            ("rqr", rot_cols(ret_q)), ("rk", ret_k), ("rkr", rot_cols(ret_k)), ("rv", ret_v),
            ("rg", ret_g), ("f", f_pad)]
    offs, o = {}, 0
    for name, c in cols:
        offs[name] = (o, o + c.shape[-1])
        o += c.shape[-1]
    return jnp.concatenate([c for _, c in cols], axis=-1).astype(BF16), offs


def _rope_tables(pos, ret_heads):
    half = HEAD_DIM // 2
    inv = ROPE_BASE ** (-jnp.arange(half, dtype=F32) / half)
    ang = pos.astype(F32)[:, None] * inv[None, :]
    cos = jnp.tile(jnp.cos(ang), (1, 2 * ret_heads))
    sin = jnp.tile(jnp.sin(ang), (1, 2 * ret_heads))
    return cos, sin


def _pair_states(s):
    b, hds = s.shape[:2]
    s = s.reshape(b, hds // 2, 2, HEAD_DIM, HEAD_DIM)
    z = jnp.zeros_like(s[:, :, 0])
    top = jnp.concatenate([s[:, :, 0], z], axis=-1)
    bot = jnp.concatenate([z, s[:, :, 1]], axis=-1)
    return jnp.concatenate([top, bot], axis=-2)


def _unpair_states(sp):
    a = sp[:, :, :HEAD_DIM, :HEAD_DIM]
    b = sp[:, :, HEAD_DIM:, HEAD_DIM:]
    return jnp.stack([a, b], axis=2).reshape(sp.shape[0], -1, HEAD_DIM, HEAD_DIM)


def _trunk(x, p, cache, wts, dims):
    batch, t, d = x.shape
    n = batch * t
    depth = wts["w_in"].shape[0]
    heads, ret_heads = dims["heads"], dims["ret_heads"]
    has_past = cache is not None
    offset = cache[0].shape[2] if has_past else 0
    pos = offset + jnp.arange(t, dtype=jnp.int32)
    cos, sin = _rope_tables(pos, ret_heads)
    if t < TOKEN_TILE:
        cos, sin = jnp.tile(cos, (batch, 1)), jnp.tile(sin, (batch, 1))

    if has_past:
        ret_blk, ret_rows = LANES, LANES
        tabs = _retention_tables(ret_heads, t, ret_blk)
    else:
        ret_blk = RET_CHUNK if t % RET_CHUNK == 0 else t
        ret_rows = t
        tabs = _retention_tables(ret_heads, ret_blk, ret_blk)

    sgu_full = wts["sgu_w"].shape[2]
    sgu_len = sgu_full if t >= sgu_full else t

    h = x.reshape(n, d)
    new_k, new_v, new_lf, new_ret, new_sgu = [], [], [], [], []
    for i in range(depth):
        outs = _inproj(h, wts["g_mix"][i], wts["w_in"][i], wts["b_forget"][i], cos, sin,
                       wts["sgu_ln_g"][i], wts["sgu_ln_b"][i], offs=wts["offs"], dims=dims,
                       seq_len=t, do_cumsum=not has_past)
        if has_past:
            qb, k32, v32, kb, vb, lf, u, sv, rq, rk, rv, rg = outs
            oa = _fox_sample(qb, kb, vb, lf, cache[0][i].reshape(batch, offset, -1),
                             cache[1][i].reshape(batch, offset, -1), cache[2][i],
                             batch=batch, ts=t, heads=heads)
        else:
            qb, k32, v32, kb, vb, lf, fc, u, sv, rq, rk, rv, rg = outs
            oa = _fox_prompt(qb, kb, vb, fc, batch=batch, seq=t, heads=heads)

        sgu_w, sgu_b = wts["sgu_w"][i], wts["sgu_b"][i]
        groups = sgu_w.shape[0]
        w_l = sgu_w[:, :sgu_len, :sgu_len]
        b_l = jnp.repeat(sgu_b[:, :sgu_len].T, HEAD_DIM, axis=1)
        if sgu_len % 128 != 0:
            reps = n // sgu_len
            eye = jnp.eye(reps, dtype=F32)
            w_l = jnp.einsum("ab,gts->gatbs", eye, w_l).reshape(groups, n, n)
            b_l = jnp.tile(b_l, (reps, 1))
        ob = _sgu(u, sv, w_l, b_l)

        if has_past:
            padr = lambda a: jnp.pad(a.reshape(batch, t, -1), ((0, 0), (0, ret_rows - t), (0, 0))
                                     ).reshape(batch * ret_rows, -1)
            s0 = _pair_states(cache[3][i].astype(F32))
            oc, sfin = _retention(padr(rq), padr(rk), padr(rv), padr(rg), s0, tabs, wts["ret_gn_g"][i],
                                  batch=batch, rows_per_seq=ret_rows, blk=ret_blk)
            oc = oc.reshape(batch, ret_rows, -1)[:, :t].reshape(n, -1)
        else:
            s0 = jnp.zeros((batch, ret_heads // 2, LANES, LANES), F32)
            oc, sfin = _retention(rq, rk, rv, rg, s0, tabs, wts["ret_gn_g"][i],
                                  batch=batch, rows_per_seq=ret_rows, blk=ret_blk)

        j = i // 2
        is_moe = i % 2 == 1
        last = i == depth - 1
        g_final = wts["g_final"] if last else None
        if is_moe:
            router = (wts["w_router"][j], wts["b_router"][j], wts["n_experts"])
            h, hn, eg = _outproj(h, oa, ob, oc, wts["w_out"][i], wts["g_ffn"][i], router)
            rb = MOE_ROW_BLOCK if n >= 8 * MOE_ROW_BLOCK else LANES
            y = _moe(hn, eg, wts["w_exp_gate"][j], wts["w_exp_up"][j], wts["w_exp_down"][j], rb=rb)
            h = _ple(h, y, p[i].reshape(n, -1), wts["g_ple"][i], wts["w_ple_gate"][i],
                     wts["w_ple_proj"][i], g_final)
        else:
            h, hn = _outproj(h, oa, ob, oc, wts["w_out"][i], wts["g_ffn"][i])
            h = _dense_ffn(h, hn, wts["w_dense_gate"][j], wts["w_dense_up"][j], wts["w_dense_down"][j])
            h = _ple(h, None, p[i].reshape(n, -1), wts["g_ple"][i], wts["w_ple_gate"][i],
                     wts["w_ple_proj"][i], g_final)

        new_k.append(k32.reshape(batch, t, heads, HEAD_DIM))
        new_v.append(v32.reshape(batch, t, heads, HEAD_DIM))
        new_lf.append(lf.reshape(batch, t, heads))
        new_ret.append(_unpair_states(sfin))
        new_sgu.append(sv.reshape(batch, t, -1))

    y = h.reshape(batch, t, d)
    return (y, jnp.stack(new_k), jnp.stack(new_v), jnp.stack(new_lf), jnp.stack(new_ret),
            jnp.stack(new_sgu))


def kernel(x_prompt, x_sample, cache_fa_k, cache_fa_v, cache_fa_logf, state_ret, p_prompt, p_sample, g_mix, w_in, b_forget, sgu_ln_g, sgu_ln_b, sgu_w, sgu_b, ret_gn_g, w_out, g_ffn, w_dense_gate, w_dense_up, w_dense_down, w_router, b_router, w_exp_gate, w_exp_up, w_exp_down, g_ple, w_ple_gate, w_ple_proj, g_final):
    heads = cache_fa_k.shape[3]
    ret_heads = state_ret.shape[2]
    groups = sgu_w.shape[1]
    n_experts = w_router.shape[-1]
    dims = dict(heads=heads, ret_heads=ret_heads, faw=heads * HEAD_DIM, sw=groups * HEAD_DIM,
                rw=ret_heads * HEAD_DIM)
    assert cache_fa_k.shape[4] == HEAD_DIM and heads % 2 == 0 and ret_heads % 2 == 0

    w_in_b, offs = _prep_w_in(w_in, dims)
    row = lambda a: a[:, None, :].astype(F32)
    wr1 = w_router.astype(BF16)
    wr2 = (w_router - wr1.astype(F32)).astype(BF16)
    pad_e = lambda a: jnp.pad(a, [(0, 0)] * (a.ndim - 1) + [(0, LANES - n_experts)])
    wts = dict(
        offs=offs, n_experts=n_experts,
        g_mix=row(g_mix), w_in=w_in_b,
        b_forget=row(jnp.pad(b_forget, ((0, 0), (0, LANES - heads)))),
        sgu_ln_g=row(sgu_ln_g), sgu_ln_b=row(sgu_ln_b), sgu_w=sgu_w, sgu_b=sgu_b,
        ret_gn_g=row(ret_gn_g), w_out=w_out.astype(BF16), g_ffn=row(g_ffn),
        w_dense_gate=w_dense_gate.astype(BF16), w_dense_up=w_dense_up.astype(BF16),
        w_dense_down=w_dense_down.astype(BF16),
        w_router=jnp.concatenate([pad_e(wr1), pad_e(wr2)], axis=-1), b_router=row(pad_e(b_router)),
        w_exp_gate=w_exp_gate.astype(BF16), w_exp_up=w_exp_up.astype(BF16),
        w_exp_down=w_exp_down.astype(BF16),
        g_ple=row(g_ple), w_ple_gate=w_ple_gate.astype(BF16), w_ple_proj=w_ple_proj.astype(BF16),
        g_final=g_final[None, :].astype(F32),
    )

    y_p, k_p, v_p, lf_p, ret_p, _ = _trunk(x_prompt, p_prompt, None, wts, dims)
    y_s, k_s, v_s, lf_s, ret_s, sgu_s = _trunk(
        x_sample, p_sample, (cache_fa_k, cache_fa_v, cache_fa_logf, state_ret), wts, dims)
    return (y_p, y_s, k_p, v_p, lf_p, ret_p, k_s, v_s, lf_s, ret_s, sgu_s)
```

```python
import functools
import math

import jax
import jax.numpy as jnp
import numpy as np
from jax import lax
from jax.experimental import pallas as pl
from jax.experimental.pallas import tpu as pltpu

F32 = jnp.float32
BF16 = jnp.bfloat16

HEAD_DIM = 64
LANES = 128
ROPE_BASE = 10000.0
RET_DECAY_MIN = 1.0 / 32.0
RET_DECAY_MAX = 1.0 / 512.0
NORM_EPS = 1e-6
GN_EPS = 1e-5
NEG_INF = -1e30
TOP_K = 2

TOKEN_TILE = 512
ATTN_TILE = 512
RET_CHUNK = 256
MOE_ROW_BLOCK = 1024
VMEM_LIMIT = 48 * 1024 * 1024


def _cparams(sem, vmem=VMEM_LIMIT):
    return pltpu.CompilerParams(dimension_semantics=sem, vmem_limit_bytes=vmem)


def _full(shape):
    zeros = (0,) * len(shape)
    return pl.BlockSpec(shape, lambda *_: zeros)


def _resident(shape):
    zeros = (0,) * len(shape)
    return pl.BlockSpec(shape, lambda *_: zeros, pipeline_mode=pl.Buffered(1))


def _rms(x, g):
    return x * lax.rsqrt(jnp.mean(x * x, axis=-1, keepdims=True) + NORM_EPS) * g


def _silu(x):
    return x * (1.0 / (1.0 + jnp.exp(-x)))


def _split3(x):
    p1 = x.astype(BF16)
    r1 = x - p1.astype(F32)
    p2 = r1.astype(BF16)
    p3 = (r1 - p2.astype(F32)).astype(BF16)
    return p1, p2, p3


def _nt_dot(a, b):
    return lax.dot_general(a, b, (((1,), (1,)), ((), ())), preferred_element_type=F32)


def _inproj_kernel(*refs, offs, tiles_per_seq, prompt, tm, heads):
    (h_ref, g_ref, w_ref, bf_ref, cos_ref, sin_ref, lng_ref, lnb_ref) = refs[:8]
    if prompt:
        (selk_ref, onek_ref, selq_ref, oneq_ref, hsel_ref) = refs[8:13]
        (qa_ref, k32_ref, v32_ref, ka_ref, vt_ref, st_ref, lf_ref, u_ref, sv_ref,
         rq_ref, rk_ref, rv_ref, rg_ref, carry_ref) = refs[13:]
    else:
        (q_ref, k32_ref, v32_ref, kb_ref, vb_ref, lf_ref, u_ref, sv_ref,
         rq_ref, rk_ref, rv_ref, rg_ref) = refs[8:]

    xn = _rms(h_ref[...], g_ref[...]).astype(BF16)

    def proj(name):
        a, b = offs[name]
        return jnp.dot(xn, w_ref[:, a:b], preferred_element_type=F32)

    q = proj("q") * (HEAD_DIM ** -0.5)
    k = proj("k")
    k32_ref[...] = k
    v = proj("v")
    v32_ref[...] = v
    if not prompt:
        q_ref[...] = q.astype(BF16)
        kb_ref[...] = k.astype(BF16)
        vb_ref[...] = v.astype(BF16)

    u_ref[...] = jax.nn.gelu(proj("u"))
    sv = jax.nn.gelu(proj("sv"))
    mu = jnp.mean(sv, axis=-1, keepdims=True)
    var = jnp.mean(jnp.square(sv - mu), axis=-1, keepdims=True)
    sv_ref[...] = (sv - mu) * lax.rsqrt(var + GN_EPS) * lng_ref[...] + lnb_ref[...]

    cos = cos_ref[...]
    sin = sin_ref[...]
    rq_ref[...] = (proj("rq") * cos + proj("rqr") * sin).astype(BF16)
    rk_ref[...] = ((proj("rk") * cos + proj("rkr") * sin) * (HEAD_DIM ** -0.5)).astype(BF16)
    rv_ref[...] = proj("rv").astype(BF16)
    rg_ref[...] = _silu(proj("rg")).astype(BF16)

    f = proj("f") + bf_ref[...]
    lf = jnp.minimum(f, 0.0) - jnp.log1p(jnp.exp(-jnp.abs(f)))
    nh = lf_ref.shape[-1]
    lf_ref[...] = lf[:, :nh]

    if prompt:
        i = pl.program_id(0)

        @pl.when(i % tiles_per_seq == 0)
        def _():
            carry_ref[...] = jnp.zeros_like(carry_ref)

        row = lax.broadcasted_iota(jnp.int32, (tm, tm), 0)
        col = lax.broadcasted_iota(jnp.int32, (tm, tm), 1)
        tril = jnp.where(col <= row, 1.0, 0.0).astype(BF16)
        pieces = jnp.concatenate(_split3(lf), axis=1)
        cs = jnp.dot(tril, pieces, preferred_element_type=F32)
        fc = cs[:, :LANES] + cs[:, LANES:2 * LANES] + cs[:, 2 * LANES:] + carry_ref[...]
        carry_ref[...] = fc[tm - 1:tm, :]

        fpieces = jnp.concatenate(_split3(fc), axis=1)
        kaug = jnp.dot(fpieces, selk_ref[...], preferred_element_type=F32) + onek_ref[...]
        qaug = jnp.dot(fpieces, selq_ref[...], preferred_element_type=F32) + oneq_ref[...]
        def head_norm2(x):
            sq = x * x
            s1 = sq.astype(BF16)
            s2 = (sq - s1.astype(F32)).astype(BF16)
            n2 = (jnp.dot(s1, hsel_ref[...], preferred_element_type=F32)
                  + jnp.dot(s2, hsel_ref[...], preferred_element_type=F32))
            return jnp.max(n2, axis=0, keepdims=True)

        st_ref[0] = jnp.concatenate([head_norm2(q), head_norm2(k), fc[0:1, :], fc[tm - 1:tm, :],
                                     jnp.zeros((4, LANES), F32)], axis=0)

        lane_head = lax.broadcasted_iota(jnp.int32, (1, LANES), 1) // HEAD_DIM
        for p in range(heads // 2):
            sl = slice(p * LANES, (p + 1) * LANES)
            ka_ref[p, :, :LANES] = k[:, sl].astype(BF16)
            ka_ref[p, :, LANES:] = kaug[:, sl].astype(BF16)
            vt_ref[p, 0] = v[:, sl].T.astype(BF16)
            for hh in range(2):
                hd = 2 * p + hh
                qa_ref[hd, :, :LANES] = jnp.where(lane_head == hh, q[:, sl], 0.0).astype(BF16)
                qa_ref[hd, :, LANES:] = qaug[:, hd * LANES:(hd + 1) * LANES].astype(BF16)


def _bias_selectors(heads):
    npairs = heads // 2
    selk = np.zeros((3 * LANES, npairs * LANES), np.float32)
    onek = np.zeros((1, npairs * LANES), np.float32)
    selq = np.zeros((3 * LANES, heads * LANES), np.float32)
    oneq = np.zeros((1, heads * LANES), np.float32)
    for p in range(npairs):
        for a in range(3):
            for hh in range(2):
                selk[a * LANES + 2 * p + hh, p * LANES + 3 * hh + a] = -1.0
            onek[0, p * LANES + 6 + a] = 1.0
    for hd in range(heads):
        for a in range(3):
            selq[a * LANES + hd, hd * LANES + 6 + a] = 1.0
            oneq[0, hd * LANES + 3 * (hd % 2) + a] = 1.0
    hsel = (np.arange(heads * HEAD_DIM)[:, None] // HEAD_DIM == np.arange(LANES)[None, :])
    return (jnp.asarray(selk, BF16), jnp.asarray(onek), jnp.asarray(selq, BF16), jnp.asarray(oneq),
            jnp.asarray(hsel, BF16))


def _inproj(h, g, w, bf, cos, sin, lng, lnb, *, offs, dims, seq_len, prompt):
    n, d = h.shape
    tm = TOKEN_TILE if n % TOKEN_TILE == 0 else n
    nt = n // tm
    tiles_per_seq = max(seq_len // tm, 1)
    tab_blocks = cos.shape[0] // tm
    faw, sw, rw, nh = dims["faw"], dims["sw"], dims["rw"], dims["heads"]
    npairs = nh // 2
    tile = lambda width: pl.BlockSpec((tm, width), lambda i: (i, 0))
    tab = pl.BlockSpec((tm, rw), lambda i: (i % tab_blocks, 0))
    in_specs = [tile(d), _full((1, d)), _resident(w.shape), _full((1, LANES)), tab, tab,
                _full((1, sw)), _full((1, sw))]
    args = [h, g, w, bf, cos, sin, lng, lnb]
    f32_rows = jax.ShapeDtypeStruct((n, faw), F32)
    bf_rows = jax.ShapeDtypeStruct((n, faw), BF16)
    if prompt:
        assert tm == ATTN_TILE
        sels = _bias_selectors(nh)
        in_specs += [_full(s.shape) for s in sels]
        args += list(sels)
        out_shape = [jax.ShapeDtypeStruct((nh, n, 2 * LANES), BF16), f32_rows, f32_rows,
                     jax.ShapeDtypeStruct((npairs, n, 2 * LANES), BF16),
                     jax.ShapeDtypeStruct((npairs, nt, LANES, tm), BF16),
                     jax.ShapeDtypeStruct((nt, 8, LANES), F32)]
        out_specs = [pl.BlockSpec((nh, tm, 2 * LANES), lambda i: (0, i, 0)), tile(faw), tile(faw),
                     pl.BlockSpec((npairs, tm, 2 * LANES), lambda i: (0, i, 0)),
                     pl.BlockSpec((npairs, 1, LANES, tm), lambda i: (0, i, 0, 0)),
                     pl.BlockSpec((1, 8, LANES), lambda i: (i, 0, 0))]
    else:
        out_shape = [bf_rows, f32_rows, f32_rows, bf_rows, bf_rows]
        out_specs = [tile(faw)] * 5
    out_shape += [jax.ShapeDtypeStruct((n, nh), F32),
                  jax.ShapeDtypeStruct((n, sw), F32), jax.ShapeDtypeStruct((n, sw), F32)]
    out_specs += [tile(nh), tile(sw), tile(sw)]
    out_shape += [jax.ShapeDtypeStruct((n, rw), BF16)] * 4
    out_specs += [tile(rw)] * 4
    scratch = [pltpu.VMEM((1, LANES), F32)] if prompt else []
    return pl.pallas_call(
        functools.partial(_inproj_kernel, offs=offs, tiles_per_seq=tiles_per_seq,
                          prompt=prompt, tm=tm, heads=nh),
        grid=(nt,), in_specs=in_specs, out_specs=out_specs, out_shape=out_shape,
        scratch_shapes=scratch, compiler_params=_cparams(("arbitrary",)),
        name="inproj",
    )(*args)


def _fox_prompt_kernel(first_ref, q_ref, k_ref, vt_ref, o_ref, m_ref, l_ref, acc_ref, *, t):
    qi = pl.program_id(2)
    first = first_ref[(pl.program_id(0) * pl.num_programs(1) + pl.program_id(1)) * pl.num_programs(2) + qi]
    m_ref[...] = jnp.full(m_ref.shape, NEG_INF, F32)
    l_ref[...] = jnp.zeros(l_ref.shape, F32)
    acc_ref[...] = jnp.zeros(acc_ref.shape, F32)

    def step(ki, masked):
        start = pl.multiple_of(ki * t, t)
        k2 = k_ref[0, pl.ds(start, t), :]
        vt2 = vt_ref[0, ki]
        for hh in range(2):
            s = _nt_dot(k2, q_ref[hh])
            if masked:
                key = lax.broadcasted_iota(jnp.int32, (t, t), 0)
                qry = lax.broadcasted_iota(jnp.int32, (t, t), 1)
                s = jnp.where(key <= qry, s, NEG_INF)
            m_prev = m_ref[hh]
            m_new = jnp.maximum(m_prev, jnp.max(s, axis=0, keepdims=True))
            alpha = jnp.exp(m_prev - m_new)
            pr = jnp.exp(s - m_new)
            l_ref[hh] = alpha * l_ref[hh] + jnp.sum(pr, axis=0, keepdims=True)
            acc_ref[hh] = alpha * acc_ref[hh] + jnp.dot(vt2, pr.astype(BF16),
                                                        preferred_element_type=F32)
            m_ref[hh] = m_new

    def body(ki, carry):
        step(ki, False)
        return carry

    lax.fori_loop(first, qi, body, 0)
    step(qi, True)
    head_a = lax.broadcasted_iota(jnp.int32, (LANES, 1), 0) < HEAD_DIM
    o_t = jnp.where(head_a, acc_ref[0] / l_ref[0], acc_ref[1] / l_ref[1])
    o_ref[...] = o_t.T.astype(BF16)


F32_EXP_ZERO = -104.0


def _first_live_tile(stats, *, batch, nq, heads):
    st = stats.reshape(batch, nq, 8, LANES)[..., :heads]
    grow = 1.0 + 2.0 ** -6
    qn = jnp.sqrt(st[:, :, 0]) * grow
    kn = jnp.sqrt(jnp.max(st[:, :, 1], axis=1, keepdims=True)) * grow
    f_first, f_last = st[:, :, 2], st[:, :, 3]
    bound = (2.0 * qn * kn)[:, :, None, :] + f_first[:, :, None, :] - f_last[:, None, :, :] + 1.0
    past = (jnp.arange(nq)[None, :] < jnp.arange(nq)[:, None])[None, :, :, None]
    dead = jnp.argmax(~(past & (bound < F32_EXP_ZERO)), axis=2).astype(jnp.int32)
    dead = jnp.min(dead.reshape(batch, nq, heads // 2, 2), axis=-1)
    return dead.transpose(0, 2, 1).reshape(-1)


def _fox_prompt(qa, ka, vt, stats, *, batch, seq, heads):
    n = qa.shape[1]
    t = ATTN_TILE
    nq = seq // t
    npairs = heads // 2
    first = _first_live_tile(stats, batch=batch, nq=nq, heads=heads)
    grid_spec = pltpu.PrefetchScalarGridSpec(
        num_scalar_prefetch=1,
        grid=(batch, npairs, nq),
        in_specs=[pl.BlockSpec((2, t, 2 * LANES), lambda b, p, i, f: (p, b * nq + i, 0)),
                  pl.BlockSpec((1, seq, 2 * LANES), lambda b, p, i, f: (p, b, 0)),
                  pl.BlockSpec((1, nq, LANES, t), lambda b, p, i, f: (p, b, 0, 0))],
        out_specs=pl.BlockSpec((t, LANES), lambda b, p, i, f: (b * nq + i, p)),
        scratch_shapes=[pltpu.VMEM((2, 1, t), F32), pltpu.VMEM((2, 1, t), F32),
                        pltpu.VMEM((2, LANES, t), F32)],
    )
    return pl.pallas_call(
        functools.partial(_fox_prompt_kernel, t=t),
        grid_spec=grid_spec,
        out_shape=jax.ShapeDtypeStruct((n, heads * HEAD_DIM), BF16),
        compiler_params=_cparams(("arbitrary", "arbitrary", "arbitrary")),
        name="fox_prompt",
    )(first, qa, ka, vt)


def _fox_sample_kernel(q_ref, kn_ref, vn_ref, pk_ref, pv_ref, plft_ref, lf_ref, lft_ref, o_ref,
                       *, heads, ts, past):
    hi = lax.Precision.HIGHEST
    r_ = lax.broadcasted_iota(jnp.int32, (past, past), 0)
    c_ = lax.broadcasted_iota(jnp.int32, (past, past), 1)
    after = jnp.where(r_ > c_, 1.0, 0.0).astype(F32)
    g_row = jnp.dot(plft_ref[0], after, precision=hi, preferred_element_type=F32)
    rr = lax.broadcasted_iota(jnp.int32, (LANES, LANES), 0)
    cc = lax.broadcasted_iota(jnp.int32, (LANES, LANES), 1)
    tril = jnp.where(cc <= rr, 1.0, 0.0).astype(F32)
    triu = jnp.where(rr <= cc, 1.0, 0.0).astype(F32)
    c_col = jnp.dot(tril, lf_ref[0], precision=hi, preferred_element_type=F32)[:ts]
    c_row = jnp.dot(lft_ref[0], triu, precision=hi, preferred_element_type=F32)
    visible = (lax.broadcasted_iota(jnp.int32, (ts, LANES), 1)
               <= lax.broadcasted_iota(jnp.int32, (ts, LANES), 0))
    lane = lax.broadcasted_iota(jnp.int32, (1, LANES), 1)
    lo = lane < HEAD_DIM
    for p in range(heads // 2):
        sl = slice(p * LANES, (p + 1) * LANES)
        q2 = q_ref[:, sl]
        kp2 = pk_ref[0, :, sl].astype(BF16)
        vp2 = pv_ref[0, :, sl].astype(BF16)
        kn2 = kn_ref[:, sl]
        vn2 = vn_ref[:, sl]
        zero = jnp.zeros_like(q2)
        res = []
        for hh in range(2):
            h = 2 * p + hh
            qh = jnp.where(lo, q2, zero) if hh == 0 else jnp.where(lo, zero, q2)
            s_p = _nt_dot(qh, kp2) + c_col[:, h:h + 1] + g_row[h:h + 1, :]
            s_n = _nt_dot(qh, kn2) + c_col[:, h:h + 1] - c_row[h:h + 1, :]
            s_n = jnp.where(visible, s_n, NEG_INF)
            m = jnp.maximum(jnp.max(s_p, axis=1, keepdims=True), jnp.max(s_n, axis=1, keepdims=True))
            pp = jnp.exp(s_p - m)
            pn = jnp.exp(s_n - m)
            l = jnp.sum(pp, axis=1, keepdims=True) + jnp.sum(pn, axis=1, keepdims=True)
            o = (jnp.dot(pp.astype(BF16), vp2, preferred_element_type=F32)
                 + jnp.dot(pn.astype(BF16), vn2, preferred_element_type=F32))
            res.append(o / l)
        o_ref[:, sl] = jnp.where(lo, res[0], res[1]).astype(BF16)


def _fox_sample(qb, kb, vb, lf, past_k, past_v, past_lf, *, batch, ts, heads):
    n, faw = qb.shape
    past = past_k.shape[1]
    plft = past_lf.transpose(0, 2, 1)
    lf3 = lf.reshape(batch, ts, heads)
    lfp = jnp.pad(lf3, ((0, 0), (0, LANES - ts), (0, LANES - heads)))
    lftp = jnp.pad(lf3.transpose(0, 2, 1), ((0, 0), (0, 0), (0, LANES - ts)))
    padr = lambda a: jnp.pad(a.reshape(batch, ts, faw), ((0, 0), (0, LANES - ts), (0, 0))
                             ).reshape(batch * LANES, faw)
    row = lambda rows: pl.BlockSpec((rows, faw), lambda b: (b, 0))
    return pl.pallas_call(
        functools.partial(_fox_sample_kernel, heads=heads, ts=ts, past=past),
        grid=(batch,),
        in_specs=[row(ts), row(LANES), row(LANES),
                  pl.BlockSpec((1, past, faw), lambda b: (b, 0, 0)),
                  pl.BlockSpec((1, past, faw), lambda b: (b, 0, 0)),
                  pl.BlockSpec((1, heads, past), lambda b: (b, 0, 0)),
                  pl.BlockSpec((1, LANES, LANES), lambda b: (b, 0, 0)),
                  pl.BlockSpec((1, heads, LANES), lambda b: (b, 0, 0))],
        out_specs=row(ts),
        out_shape=jax.ShapeDtypeStruct((n, faw), BF16),
        compiler_params=_cparams(("arbitrary",)),
        name="fox_sample",
    )(qb, padr(kb), padr(vb), past_k, past_v, plft, lfp, lftp)


def _sgu_kernel(u_ref, v_ref, w_ref, b_ref, o_ref, *, chunk, chunks_per_tile, groups):
    width = u_ref.shape[-1]
    row = lax.broadcasted_iota(jnp.int32, (chunk, chunk), 0)
    col = lax.broadcasted_iota(jnp.int32, (chunk, chunk), 1)
    causal = col <= row
    grp = lax.broadcasted_iota(jnp.int32, (1, width), 1) // HEAD_DIM
    wm = [jnp.where(causal, w_ref[g], 0.0).astype(BF16) for g in range(groups)]
    bias = b_ref[...]
    for c in range(chunks_per_tile):
        sl = slice(c * chunk, (c + 1) * chunk)
        vc = v_ref[sl, :].astype(BF16)
        mixed = jnp.zeros((chunk, width), F32)
        for g in range(groups):
            mixed = jnp.where(grp == g, jnp.dot(wm[g], vc, preferred_element_type=F32), mixed)
        o_ref[sl, :] = (u_ref[sl, :] * (mixed + bias)).astype(BF16)


def _sgu(u, v, w, bias):
    n, width = u.shape
    groups, chunk, _ = w.shape
    tile = TOKEN_TILE if (n % TOKEN_TILE == 0 and TOKEN_TILE % chunk == 0) else chunk
    tok = pl.BlockSpec((tile, width), lambda i: (i, 0))
    return pl.pallas_call(
        functools.partial(_sgu_kernel, chunk=chunk, chunks_per_tile=tile // chunk, groups=groups),
        grid=(n // tile,),
        in_specs=[tok, tok, _full(w.shape), _full(bias.shape)],
        out_specs=tok,
        out_shape=jax.ShapeDtypeStruct((n, width), BF16),
        compiler_params=_cparams(("arbitrary",)),
        name="sgu",
    )(u, v, w, bias)


def _retention_kernel(q_ref, k_ref, v_ref, g_ref, s0_ref, dmat_ref, qdec_ref, kdec_ref, sdec_ref,
                      bmask_ref, avg_ref, gn_ref, o_ref, sfin_ref, s_ref, *, npairs):
    c = pl.program_id(1)

    @pl.when(c == 0)
    def _():
        s_ref[...] = s0_ref[0]

    lane = lax.broadcasted_iota(jnp.int32, (1, LANES), 1)
    lo = lane < HEAD_DIM
    avg = avg_ref[...]

    def group_mean(x):
        x1 = x.astype(BF16)
        x2 = (x - x1.astype(F32)).astype(BF16)
        return (jnp.dot(x1, avg, preferred_element_type=F32)
                + jnp.dot(x2, avg, preferred_element_type=F32))

    for p in range(npairs):
        sl = slice(p * LANES, (p + 1) * LANES)
        q2 = q_ref[:, sl]
        k2 = k_ref[:, sl]
        v2 = v_ref[:, sl]
        zero = jnp.zeros_like(q2)
        inner = []
        for hh in range(2):
            qh = jnp.where(lo, q2, zero) if hh == 0 else jnp.where(lo, zero, q2)
            sc = _nt_dot(qh, k2) * dmat_ref[2 * p + hh]
            inner.append(jnp.dot(sc.astype(BF16), v2, preferred_element_type=F32))
        state = s_ref[p]
        cross = jnp.dot(q2, state.astype(BF16), preferred_element_type=F32) * qdec_ref[:, sl]
        o = jnp.where(lo, inner[0], inner[1]) + cross
        kd_t = (k2.astype(F32) * kdec_ref[:, sl]).T.astype(BF16)
        s_ref[p] = state * sdec_ref[p] + jnp.dot(kd_t, v2, preferred_element_type=F32) * bmask_ref[...]
        mu = group_mean(o)
        d = o - mu
        var = group_mean(d * d)
        on = d * lax.rsqrt(var + GN_EPS)
        o_ref[:, sl] = (on * gn_ref[:, sl] * g_ref[:, sl].astype(F32)).astype(BF16)

    sfin_ref[0] = s_ref[...]


def _retention(rq, rk, rv, rg, s0, tabs, gn, *, batch, rows_per_seq, blk):
    n, rw = rq.shape
    npairs = rw // LANES
    nc = rows_per_seq // blk
    tok = pl.BlockSpec((blk, rw), lambda b, c: (b * nc + c, 0))
    st = pl.BlockSpec((1, npairs, LANES, LANES), lambda b, c: (b, 0, 0, 0))
    dmat, qdec, kdec, sdec, bmask, avg = tabs
    return pl.pallas_call(
        functools.partial(_retention_kernel, npairs=npairs),
        grid=(batch, nc),
        in_specs=[tok, tok, tok, tok, st, _full(dmat.shape), _full(qdec.shape), _full(kdec.shape),
                  _full(sdec.shape), _full(bmask.shape), _full(avg.shape), _full(gn.shape)],
        out_specs=[tok, st],
        out_shape=[jax.ShapeDtypeStruct((n, rw), BF16),
                   jax.ShapeDtypeStruct((batch, npairs, LANES, LANES), F32)],
        scratch_shapes=[pltpu.VMEM((npairs, LANES, LANES), F32)],
        compiler_params=_cparams(("arbitrary", "arbitrary")),
        name="retention",
    )(rq, rk, rv, rg, s0, dmat, qdec, kdec, sdec, bmask, avg, gn)


def _retention_tables(ret_heads, true_len, blk):
    log_g = jnp.log1p(-jnp.exp(jnp.linspace(math.log(RET_DECAY_MIN), math.log(RET_DECAY_MAX),
                                            ret_heads, dtype=F32)))
    n = jnp.arange(blk, dtype=F32)
    diff = n[:, None] - n[None, :]
    causal = diff >= 0
    dmat = jnp.where(causal[None], jnp.exp(jnp.where(causal, diff, 0.0)[None] * log_g[:, None, None]), 0.0)
    per_lane = jnp.repeat(log_g, HEAD_DIM)
    qdec = jnp.exp((n[:, None] + 1.0) * per_lane[None, :])
    kdec = jnp.exp((true_len - 1.0 - n)[:, None] * per_lane[None, :])
    npairs = ret_heads // 2
    lane_head = jnp.arange(LANES) // HEAD_DIM
    bmask = (lane_head[:, None] == lane_head[None, :]).astype(F32)
    sdec = jnp.exp(true_len * per_lane).reshape(npairs, LANES)[:, :, None] * bmask[None]
    avg = (bmask / HEAD_DIM).astype(BF16)
    return dmat, qdec, kdec, sdec, bmask, avg


def _outproj_kernel(*refs, with_router, n_experts, faw, sw):
    if with_router:
        (h_ref, oa_ref, ob_ref, oc_ref, w_ref, g_ref, wr_ref, br_ref, hnew_ref, hn_ref, eg_ref) = refs
    else:
        (h_ref, oa_ref, ob_ref, oc_ref, w_ref, g_ref, hnew_ref, hn_ref) = refs
    o = (jnp.dot(oa_ref[...], w_ref[:faw, :], preferred_element_type=F32)
         + jnp.dot(ob_ref[...], w_ref[faw:faw + sw, :], preferred_element_type=F32)
         + jnp.dot(oc_ref[...], w_ref[faw + sw:, :], preferred_element_type=F32))
    h = h_ref[...] + o
    hnew_ref[...] = h
    hn = _rms(h, g_ref[...])
    hn_ref[...] = hn.astype(BF16)
    if with_router:
        x1 = hn.astype(BF16)
        x2 = (hn - x1.astype(F32)).astype(BF16)
        a = jnp.dot(x1, wr_ref[...], preferred_element_type=F32)
        b = jnp.dot(x2, wr_ref[:, :LANES], preferred_element_type=F32)
        logits = a[:, :LANES] + a[:, LANES:] + b + br_ref[...]
        lane = lax.broadcasted_iota(jnp.int32, logits.shape, 1)
        lanef = lane.astype(F32)
        logits = jnp.where(lane < n_experts, logits, -jnp.inf)
        m1 = jnp.max(logits, axis=1, keepdims=True)
        i1 = jnp.min(jnp.where(logits == m1, lanef, float(LANES)), axis=1, keepdims=True)
        rest = jnp.where(lanef == i1, -jnp.inf, logits)
        m2 = jnp.max(rest, axis=1, keepdims=True)
        i2 = jnp.min(jnp.where(rest == m2, lanef, float(LANES)), axis=1, keepdims=True)
        e2 = jnp.exp(m2 - m1)
        g1 = 1.0 / (1.0 + e2)
        g2 = e2 / (1.0 + e2)
        eg_ref[...] = jnp.where(lane == 0, g1, jnp.where(lane == 1, g2,
                                jnp.where(lane == 2, i1, jnp.where(lane == 3, i2, 0.0))))


def _outproj(h, oa, ob, oc, w, g, router=None):
    n, d = h.shape
    tm = TOKEN_TILE if n % TOKEN_TILE == 0 else n
    faw, sw, rw = oa.shape[1], ob.shape[1], oc.shape[1]
    tile = lambda width: pl.BlockSpec((tm, width), lambda i: (i, 0))
    in_specs = [tile(d), tile(faw), tile(sw), tile(rw), _resident(w.shape), _full((1, d))]
    args = [h, oa, ob, oc, w, g]
    out_shape = [jax.ShapeDtypeStruct((n, d), F32), jax.ShapeDtypeStruct((n, d), BF16)]
    out_specs = [tile(d), tile(d)]
    n_experts = 0
    if router is not None:
        wr, br, n_experts = router
        in_specs += [_full(wr.shape), _full(br.shape)]
        args += [wr, br]
        out_shape.append(jax.ShapeDtypeStruct((n, LANES), F32))
        out_specs.append(tile(LANES))
    return pl.pallas_call(
        functools.partial(_outproj_kernel, with_router=router is not None, n_experts=n_experts,
                          faw=faw, sw=sw),
        grid=(n // tm,), in_specs=in_specs, out_specs=out_specs, out_shape=out_shape,
        compiler_params=_cparams(("arbitrary",)),
        name="outproj",
    )(*args)


def _dense_ffn_kernel(h_ref, x_ref, wg_ref, wu_ref, wd_ref, o_ref, acc_ref, *, fc, nchunks):
    x = x_ref[...]
    for c in range(nchunks):
        sl = slice(c * fc, (c + 1) * fc)
        g = jnp.dot(x, wg_ref[:, sl], preferred_element_type=F32)
        u = jnp.dot(x, wu_ref[:, sl], preferred_element_type=F32)
        a = (_silu(g) * u).astype(BF16)
        y = jnp.dot(a, wd_ref[sl, :], preferred_element_type=F32)
        if c == 0:
            acc_ref[...] = y
        else:
            acc_ref[...] += y
    o_ref[...] = h_ref[...] + acc_ref[...]


def _dense_ffn(h, hn, wg, wu, wd):
    n, d = h.shape
    ff = wg.shape[1]
    tm = TOKEN_TILE if n % TOKEN_TILE == 0 else n
    fc = 256 if ff % 256 == 0 else ff
    tile = pl.BlockSpec((tm, d), lambda i: (i, 0))
    return pl.pallas_call(
        functools.partial(_dense_ffn_kernel, fc=fc, nchunks=ff // fc),
        grid=(n // tm,),
        in_specs=[tile, tile, _resident(wg.shape), _resident(wu.shape), _resident(wd.shape)],
        out_specs=tile,
        out_shape=jax.ShapeDtypeStruct((n, d), F32),
        scratch_shapes=[pltpu.VMEM((tm, d), F32)],
        compiler_params=_cparams(("arbitrary",)),
        name="dense_ffn",
    )(h, hn, wg, wu, wd)


def _moe_ffn_kernel(be_ref, nu_ref, x_ref, gate_ref, wg_ref, wu_ref, wd_ref, o_ref, acc_ref, *, nch):
    b = pl.program_id(0)
    c = pl.program_id(1)

    @pl.when(b < nu_ref[0])
    def _():
        x = x_ref[...]
        g = jnp.dot(x, wg_ref[0], preferred_element_type=F32)
        u = jnp.dot(x, wu_ref[0], preferred_element_type=F32)
        a = (_silu(g) * u).astype(BF16)
        y = jnp.dot(a, wd_ref[0], preferred_element_type=F32)

        @pl.when(c == 0)
        def _():
            acc_ref[...] = y

        @pl.when(c > 0)
        def _():
            acc_ref[...] += y

        @pl.when(c == nch - 1)
        def _():
            o_ref[...] = (acc_ref[...] * gate_ref[...]).astype(BF16)


def _moe_ffn(xs, row_gate, block_expert, n_used, wg, wu, wd, *, rb):
    r, d = xs.shape
    ff = wg.shape[2]
    nblk = r // rb
    nch = 4 if ff % (4 * LANES) == 0 else 1
    fc = ff // nch

    def live(b, nu):
        return jnp.minimum(b, nu[0] - 1)

    def chunk(b, c, nu):
        return jnp.where(b < nu[0], c, nch - 1)

    grid_spec = pltpu.PrefetchScalarGridSpec(
        num_scalar_prefetch=2,
        grid=(nblk, nch),
        in_specs=[pl.BlockSpec((rb, d), lambda b, c, be, nu: (live(b, nu), 0)),
                  pl.BlockSpec((rb, 1), lambda b, c, be, nu: (live(b, nu), 0)),
                  pl.BlockSpec((1, d, fc), lambda b, c, be, nu: (be[live(b, nu)], 0, chunk(b, c, nu))),
                  pl.BlockSpec((1, d, fc), lambda b, c, be, nu: (be[live(b, nu)], 0, chunk(b, c, nu))),
                  pl.BlockSpec((1, fc, d), lambda b, c, be, nu: (be[live(b, nu)], chunk(b, c, nu), 0))],
        out_specs=pl.BlockSpec((rb, d), lambda b, c, be, nu: (live(b, nu), 0)),
        scratch_shapes=[pltpu.VMEM((rb, d), F32)],
    )
    return pl.pallas_call(
        functools.partial(_moe_ffn_kernel, nch=nch),
        grid_spec=grid_spec,
        out_shape=jax.ShapeDtypeStruct((r, d), BF16),
        compiler_params=_cparams(("arbitrary", "arbitrary")),
        name="moe_ffn",
    )(block_expert, n_used, xs, row_gate, wg, wu, wd)


def _gather_rows(x, idx):
    return x.at[idx].get(mode="promise_in_bounds")


def _moe(hn, eg, wg, wu, wd, *, rb):
    n, d = hn.shape
    n_experts = wg.shape[0]
    gates = eg[:, :TOP_K]
    experts = eg[:, TOP_K:2 * TOP_K].astype(jnp.int32)
    flat_e = experts.reshape(-1)
    onehot = (flat_e[:, None] == jnp.arange(n_experts, dtype=jnp.int32)[None, :]).astype(jnp.int32)
    rank = jnp.cumsum(onehot, axis=0) - onehot
    count = jnp.sum(onehot, axis=0)
    padded = ((count + rb - 1) // rb) * rb
    ends = jnp.cumsum(padded)
    offset = ends - padded
    pos = offset[flat_e] + jnp.take_along_axis(rank, flat_e[:, None], axis=1)[:, 0]
    r_tot = TOP_K * n + n_experts * rb
    nblk = r_tot // rb
    src_tok = jnp.zeros((r_tot,), jnp.int32).at[pos].set(
        jnp.arange(TOP_K * n, dtype=jnp.int32) // TOP_K, unique_indices=True, mode="promise_in_bounds")
    row_gate = jnp.zeros((r_tot,), F32).at[pos].set(
        gates.reshape(-1), unique_indices=True, mode="promise_in_bounds")
    block_start = jnp.arange(nblk, dtype=jnp.int32) * rb
    block_expert = jnp.minimum(jnp.sum((block_start[:, None] >= ends[None, :]).astype(jnp.int32), axis=1),
                               n_experts - 1).astype(jnp.int32)
    n_used = (ends[-1:] // rb).astype(jnp.int32)
    xs = _gather_rows(hn, src_tok)
    ys = _moe_ffn(xs, row_gate[:, None], block_expert, n_used, wg, wu, wd, rb=rb)
    return _gather_rows(ys, pos).reshape(n, TOP_K * d)


def _ple_kernel(*refs, with_moe, final, d):
    refs = list(refs)
    h_ref = refs.pop(0)
    y_ref = refs.pop(0) if with_moe else None
    p_ref, g_ref, wg_ref, wp_ref = refs[:4]
    refs = refs[4:]
    gf_ref = refs.pop(0) if final else None
    o_ref = refs[0]
    h = h_ref[...]
    if with_moe:
        h = h + y_ref[:, :d].astype(F32) + y_ref[:, d:].astype(F32)
    gate = jax.nn.sigmoid(jnp.dot(_rms(h, g_ref[...]).astype(BF16), wg_ref[...],
                                  preferred_element_type=F32))
    h = h + gate * jnp.dot(p_ref[...].astype(BF16), wp_ref[...], preferred_element_type=F32)
    if final:
        h = _rms(h, gf_ref[...])
    o_ref[...] = h


def _ple(h, y, p, g, wg, wp, g_final=None):
    n, d = h.shape
    tm = TOKEN_TILE if n % TOKEN_TILE == 0 else n
    tile = lambda width: pl.BlockSpec((tm, width), lambda i: (i, 0))
    in_specs, args = [tile(d)], [h]
    if y is not None:
        in_specs.append(tile(y.shape[1]))
        args.append(y)
    in_specs += [tile(p.shape[1]), _full((1, d)), _resident(wg.shape), _resident(wp.shape)]
    args += [p, g, wg, wp]
    if g_final is not None:
        in_specs.append(_full((1, d)))
        args.append(g_final)
    return pl.pallas_call(
        functools.partial(_ple_kernel, with_moe=y is not None, final=g_final is not None, d=d),
        grid=(n // tm,), in_specs=in_specs, out_specs=tile(d),
        out_shape=jax.ShapeDtypeStruct((n, d), F32),
        compiler_params=_cparams(("arbitrary",)),
        name="ple",
    )(*args)


def _prep_w_in(w_in, dims):
    faw, sw, rw, nh = dims["faw"], dims["sw"], dims["rw"], dims["heads"]
    sizes = (faw, faw, faw, nh, sw, sw, rw, rw, rw, rw)
    parts, o = [], 0
    for s in sizes:
        parts.append(w_in[..., o:o + s])
        o += s
    fa_q, fa_k, fa_v, fa_f, sgu_u, sgu_v, ret_q, ret_k, ret_v, ret_g = parts

    def rot_cols(w):
        lead = w.shape[:-1]
        wh = w.reshape(lead + (rw // HEAD_DIM, 2, HEAD_DIM // 2))
        return jnp.stack([-wh[..., 1, :], wh[..., 0, :]], axis=-2).reshape(lead + (rw,))

    f_pad = jnp.pad(fa_f, [(0, 0)] * (w_in.ndim - 1) + [(0, LANES - nh)])
    cols = [("q", fa_q), ("k", fa_k), ("v", fa_v), ("u", sgu_u), ("sv", sgu_v), ("rq", ret_q),
            ("rqr", rot_cols(ret_q)), ("rk", ret_k), ("rkr", rot_cols(ret_k)), ("rv", ret_v),
            ("rg", ret_g), ("f", f_pad)]
    offs, o = {}, 0
    for name, c in cols:
        offs[name] = (o, o + c.shape[-1])
        o += c.shape[-1]
    return jnp.concatenate([c for _, c in cols], axis=-1).astype(BF16), offs


def _rope_tables(pos, ret_heads):
    half = HEAD_DIM // 2
    inv = ROPE_BASE ** (-jnp.arange(half, dtype=F32) / half)
    ang = pos.astype(F32)[:, None] * inv[None, :]
    cos = jnp.tile(jnp.cos(ang), (1, 2 * ret_heads))
    sin = jnp.tile(jnp.sin(ang), (1, 2 * ret_heads))
    return cos, sin


def _pair_states(s):
    b, hds = s.shape[:2]
    s = s.reshape(b, hds // 2, 2, HEAD_DIM, HEAD_DIM)
    z = jnp.zeros_like(s[:, :, 0])
    top = jnp.concatenate([s[:, :, 0], z], axis=-1)
    bot = jnp.concatenate([z, s[:, :, 1]], axis=-1)
    return jnp.concatenate([top, bot], axis=-2)


def _unpair_states(sp):
    a = sp[:, :, :HEAD_DIM, :HEAD_DIM]
    b = sp[:, :, HEAD_DIM:, HEAD_DIM:]
    return jnp.stack([a, b], axis=2).reshape(sp.shape[0], -1, HEAD_DIM, HEAD_DIM)


def _trunk(x, p, cache, wts, dims):
    batch, t, d = x.shape
    n = batch * t
    depth = wts["w_in"].shape[0]
    heads, ret_heads = dims["heads"], dims["ret_heads"]
    has_past = cache is not None
    offset = cache[0].shape[2] if has_past else 0
    pos = offset + jnp.arange(t, dtype=jnp.int32)
    cos, sin = _rope_tables(pos, ret_heads)
    if t < TOKEN_TILE:
        cos, sin = jnp.tile(cos, (batch, 1)), jnp.tile(sin, (batch, 1))

    if has_past:
        ret_blk, ret_rows = LANES, LANES
        tabs = _retention_tables(ret_heads, t, ret_blk)
    else:
        ret_blk = RET_CHUNK if t % RET_CHUNK == 0 else t
        ret_rows = t
        tabs = _retention_tables(ret_heads, ret_blk, ret_blk)

    sgu_full = wts["sgu_w"].shape[2]
    sgu_len = sgu_full if t >= sgu_full else t

    h = x.reshape(n, d)
    new_k, new_v, new_lf, new_ret, new_sgu = [], [], [], [], []
    for i in range(depth):
        outs = _inproj(h, wts["g_mix"][i], wts["w_in"][i], wts["b_forget"][i], cos, sin,
                       wts["sgu_ln_g"][i], wts["sgu_ln_b"][i], offs=wts["offs"], dims=dims,
                       seq_len=t, prompt=not has_past)
        if has_past:
            qb, k32, v32, kb, vb, lf, u, sv, rq, rk, rv, rg = outs
            oa = _fox_sample(qb, kb, vb, lf, cache[0][i].reshape(batch, offset, -1),
                             cache[1][i].reshape(batch, offset, -1), cache[2][i],
                             batch=batch, ts=t, heads=heads)
        else:
            qa, k32, v32, ka, vt, stats, lf, u, sv, rq, rk, rv, rg = outs
            oa = _fox_prompt(qa, ka, vt, stats, batch=batch, seq=t, heads=heads)

        sgu_w, sgu_b = wts["sgu_w"][i], wts["sgu_b"][i]
        groups = sgu_w.shape[0]
        w_l = sgu_w[:, :sgu_len, :sgu_len]
        b_l = jnp.repeat(sgu_b[:, :sgu_len].T, HEAD_DIM, axis=1)
        if sgu_len % 128 != 0:
            reps = n // sgu_len
            eye = jnp.eye(reps, dtype=F32)
            w_l = jnp.einsum("ab,gts->gatbs", eye, w_l).reshape(groups, n, n)
            b_l = jnp.tile(b_l, (reps, 1))
        ob = _sgu(u, sv, w_l, b_l)

        if has_past:
            padr = lambda a: jnp.pad(a.reshape(batch, t, -1), ((0, 0), (0, ret_rows - t), (0, 0))
                                     ).reshape(batch * ret_rows, -1)
            s0 = _pair_states(cache[3][i].astype(F32))
            oc, sfin = _retention(padr(rq), padr(rk), padr(rv), padr(rg), s0, tabs, wts["ret_gn_g"][i],
                                  batch=batch, rows_per_seq=ret_rows, blk=ret_blk)
            oc = oc.reshape(batch, ret_rows, -1)[:, :t].reshape(n, -1)
        else:
            s0 = jnp.zeros((batch, ret_heads // 2, LANES, LANES), F32)
            oc, sfin = _retention(rq, rk, rv, rg, s0, tabs, wts["ret_gn_g"][i],
                                  batch=batch, rows_per_seq=ret_rows, blk=ret_blk)

        j = i // 2
        is_moe = i % 2 == 1
        last = i == depth - 1
        g_final = wts["g_final"] if last else None
        if is_moe:
            router = (wts["w_router"][j], wts["b_router"][j], wts["n_experts"])
            h, hn, eg = _outproj(h, oa, ob, oc, wts["w_out"][i], wts["g_ffn"][i], router)
            rb = MOE_ROW_BLOCK if n >= 8 * MOE_ROW_BLOCK else LANES
            y = _moe(hn, eg, wts["w_exp_gate"][j], wts["w_exp_up"][j], wts["w_exp_down"][j], rb=rb)
            h = _ple(h, y, p[i].reshape(n, -1), wts["g_ple"][i], wts["w_ple_gate"][i],
                     wts["w_ple_proj"][i], g_final)
        else:
            h, hn = _outproj(h, oa, ob, oc, wts["w_out"][i], wts["g_ffn"][i])
            h = _dense_ffn(h, hn, wts["w_dense_gate"][j], wts["w_dense_up"][j], wts["w_dense_down"][j])
            h = _ple(h, None, p[i].reshape(n, -1), wts["g_ple"][i], wts["w_ple_gate"][i],
                     wts["w_ple_proj"][i], g_final)

        new_k.append(k32.reshape(batch, t, heads, HEAD_DIM))
        new_v.append(v32.reshape(batch, t, heads, HEAD_DIM))
        new_lf.append(lf.reshape(batch, t, heads))
        new_ret.append(_unpair_states(sfin))
        new_sgu.append(sv.reshape(batch, t, -1))

    y = h.reshape(batch, t, d)
    return (y, jnp.stack(new_k), jnp.stack(new_v), jnp.stack(new_lf), jnp.stack(new_ret),
            jnp.stack(new_sgu))


def kernel(x_prompt, x_sample, cache_fa_k, cache_fa_v, cache_fa_logf, state_ret, p_prompt, p_sample, g_mix, w_in, b_forget, sgu_ln_g, sgu_ln_b, sgu_w, sgu_b, ret_gn_g, w_out, g_ffn, w_dense_gate, w_dense_up, w_dense_down, w_router, b_router, w_exp_gate, w_exp_up, w_exp_down, g_ple, w_ple_gate, w_ple_proj, g_final):
    heads = cache_fa_k.shape[3]
    ret_heads = state_ret.shape[2]
    groups = sgu_w.shape[1]
    n_experts = w_router.shape[-1]
    dims = dict(heads=heads, ret_heads=ret_heads, faw=heads * HEAD_DIM, sw=groups * HEAD_DIM,
                rw=ret_heads * HEAD_DIM)
    assert cache_fa_k.shape[4] == HEAD_DIM and heads % 2 == 0 and ret_heads % 2 == 0

    w_in_b, offs = _prep_w_in(w_in, dims)
    row = lambda a: a[:, None, :].astype(F32)
    wr1 = w_router.astype(BF16)
    wr2 = (w_router - wr1.astype(F32)).astype(BF16)
    pad_e = lambda a: jnp.pad(a, [(0, 0)] * (a.ndim - 1) + [(0, LANES - n_experts)])
    wts = dict(
        offs=offs, n_experts=n_experts,
        g_mix=row(g_mix), w_in=w_in_b,
        b_forget=row(jnp.pad(b_forget, ((0, 0), (0, LANES - heads)))),
        sgu_ln_g=row(sgu_ln_g), sgu_ln_b=row(sgu_ln_b), sgu_w=sgu_w, sgu_b=sgu_b,
        ret_gn_g=row(ret_gn_g), w_out=w_out.astype(BF16), g_ffn=row(g_ffn),
        w_dense_gate=w_dense_gate.astype(BF16), w_dense_up=w_dense_up.astype(BF16),
        w_dense_down=w_dense_down.astype(BF16),
        w_router=jnp.concatenate([pad_e(wr1), pad_e(wr2)], axis=-1), b_router=row(pad_e(b_router)),
        w_exp_gate=w_exp_gate.astype(BF16), w_exp_up=w_exp_up.astype(BF16),
        w_exp_down=w_exp_down.astype(BF16),
        g_ple=row(g_ple), w_ple_gate=w_ple_gate.astype(BF16), w_ple_proj=w_ple_proj.astype(BF16),
        g_final=g_final[None, :].astype(F32),
    )

    y_p, k_p, v_p, lf_p, ret_p, _ = _trunk(x_prompt, p_prompt, None, wts, dims)
    y_s, k_s, v_s, lf_s, ret_s, sgu_s = _trunk(
        x_sample, p_sample, (cache_fa_k, cache_fa_v, cache_fa_logf, state_ret), wts, dims)
    return (y_p, y_s, k_p, v_p, lf_p, ret_p, k_s, v_s, lf_s, ret_s, sgu_s)
```

```python
import functools
import math

import jax
import jax.numpy as jnp
import numpy as np
from jax import lax
from jax.experimental import pallas as pl
from jax.experimental.pallas import tpu as pltpu

F32 = jnp.float32
BF16 = jnp.bfloat16

HEAD_DIM = 64
LANES = 128
ROPE_BASE = 10000.0
RET_DECAY_MIN = 1.0 / 32.0
RET_DECAY_MAX = 1.0 / 512.0
NORM_EPS = 1e-6
GN_EPS = 1e-5
NEG_INF = -1e30
TOP_K = 2
LOG2E = 1.4426950408889634

TOKEN_TILE = 512
ATTN_TILE = 512
RET_CHUNK = 256
MOE_ROW_BLOCK = 1024
VMEM_LIMIT = 48 * 1024 * 1024
MOE_VMEM_LIMIT = 56 * 1024 * 1024


def _cparams(sem, vmem=VMEM_LIMIT):
    return pltpu.CompilerParams(dimension_semantics=sem, vmem_limit_bytes=vmem)


def _full(shape):
    zeros = (0,) * len(shape)
    return pl.BlockSpec(shape, lambda *_: zeros)


def _resident(shape):
    zeros = (0,) * len(shape)
    return pl.BlockSpec(shape, lambda *_: zeros, pipeline_mode=pl.Buffered(1))


def _rms(x, g):
    return x * lax.rsqrt(jnp.mean(x * x, axis=-1, keepdims=True) + NORM_EPS) * g


def _silu(x):
    return x * (1.0 / (1.0 + jnp.exp(-x)))


def _split3(x):
    p1 = x.astype(BF16)
    r1 = x - p1.astype(F32)
    p2 = r1.astype(BF16)
    p3 = (r1 - p2.astype(F32)).astype(BF16)
    return p1, p2, p3


def _nt_dot(a, b):
    return lax.dot_general(a, b, (((1,), (1,)), ((), ())), preferred_element_type=F32)


def _inproj_kernel(*refs, offs, tiles_per_seq, prompt, tm, heads):
    (h_ref, g_ref, w_ref, bf_ref, cos_ref, sin_ref, lng_ref, lnb_ref) = refs[:8]
    if prompt:
        (selk_ref, onek_ref, selq_ref, oneq_ref, hsel_ref) = refs[10:15]
        (k32_ref, v32_ref, qa_ref, ka_ref, vt_ref, st_ref, lf_ref, u_ref, sv_ref,
         rq_ref, rk_ref, rv_ref, rg_ref, carry_ref) = refs[15:]
    else:
        (k32_ref, v32_ref, q_ref, kb_ref, vb_ref, lf_ref, u_ref, sv_ref,
         rq_ref, rk_ref, rv_ref, rg_ref) = refs[10:]

    xn = _rms(h_ref[...], g_ref[...]).astype(BF16)

    def proj(name):
        a, b = offs[name]
        return jnp.dot(xn, w_ref[:, a:b], preferred_element_type=F32)

    q = proj("q") * (HEAD_DIM ** -0.5)
    k = proj("k")
    k32_ref[0] = k
    v = proj("v")
    v32_ref[0] = v
    if not prompt:
        q_ref[...] = q.astype(BF16)
        kb_ref[...] = k.astype(BF16)
        vb_ref[...] = v.astype(BF16)

    u_ref[...] = jax.nn.gelu(proj("u"))
    sv = jax.nn.gelu(proj("sv"))
    mu = jnp.mean(sv, axis=-1, keepdims=True)
    var = jnp.mean(jnp.square(sv - mu), axis=-1, keepdims=True)
    sv_ref[...] = (sv - mu) * lax.rsqrt(var + GN_EPS) * lng_ref[...] + lnb_ref[...]

    cos = cos_ref[...]
    sin = sin_ref[...]
    rq_ref[...] = (proj("rq") * cos + proj("rqr") * sin).astype(BF16)
    rk_ref[...] = ((proj("rk") * cos + proj("rkr") * sin) * (HEAD_DIM ** -0.5)).astype(BF16)
    rv_ref[...] = proj("rv").astype(BF16)
    rg_ref[...] = _silu(proj("rg")).astype(BF16)

    f = proj("f") + bf_ref[...]
    lf = jnp.minimum(f, 0.0) - jnp.log1p(jnp.exp(-jnp.abs(f)))
    nh = lf_ref.shape[-1]
    lf_ref[...] = lf[:, :nh]

    if prompt:
        i = pl.program_id(0)

        @pl.when(i % tiles_per_seq == 0)
        def _():
            carry_ref[...] = jnp.zeros_like(carry_ref)

        row = lax.broadcasted_iota(jnp.int32, (tm, tm), 0)
        col = lax.broadcasted_iota(jnp.int32, (tm, tm), 1)
        tril = jnp.where(col <= row, 1.0, 0.0).astype(BF16)
        pieces = jnp.concatenate(_split3(lf), axis=1)
        cs = jnp.dot(tril, pieces, preferred_element_type=F32)
        fc = cs[:, :LANES] + cs[:, LANES:2 * LANES] + cs[:, 2 * LANES:] + carry_ref[...]
        carry_ref[...] = fc[tm - 1:tm, :]

        fpieces = jnp.concatenate(_split3(fc * LOG2E), axis=1)
        kaug = jnp.dot(fpieces, selk_ref[...], preferred_element_type=F32) + onek_ref[...]
        qaug = jnp.dot(fpieces, selq_ref[...], preferred_element_type=F32) + oneq_ref[...]

        sq = jnp.concatenate([q * q, k * k], axis=1) * (1.0 + 2.0 ** -7)
        n2 = jnp.max(jnp.dot(sq.astype(BF16), hsel_ref[...], preferred_element_type=F32),
                     axis=0, keepdims=True)
        st_ref[0] = jnp.concatenate([n2[:, :LANES], n2[:, LANES:], fc[0:1, :], fc[tm - 1:tm, :],
                                     jnp.zeros((4, LANES), F32)], axis=0)

        lane_head = lax.broadcasted_iota(jnp.int32, (1, LANES), 1) // HEAD_DIM
        q2 = q * LOG2E
        for p in range(heads // 2):
            sl = slice(p * LANES, (p + 1) * LANES)
            ka_ref[p, :, :LANES] = k[:, sl].astype(BF16)
            ka_ref[p, :, LANES:] = kaug[:, sl].astype(BF16)
            for hh in range(2):
                hd = 2 * p + hh
                own = lane_head == hh
                qa_ref[hd, :, :LANES] = jnp.where(own, q2[:, sl], 0.0).astype(BF16)
                qa_ref[hd, :, LANES:] = qaug[:, hd * LANES:(hd + 1) * LANES].astype(BF16)
                vt_ref[hd, 0] = jnp.where(own, v[:, sl], 1.0).T.astype(BF16)


def _bias_selectors(heads):
    npairs = heads // 2
    selk = np.zeros((3 * LANES, npairs * LANES), np.float32)
    onek = np.zeros((1, npairs * LANES), np.float32)
    selq = np.zeros((3 * LANES, heads * LANES), np.float32)
    oneq = np.zeros((1, heads * LANES), np.float32)
    for p in range(npairs):
        for a in range(3):
            for hh in range(2):
                selk[a * LANES + 2 * p + hh, p * LANES + 3 * hh + a] = -1.0
            onek[0, p * LANES + 6 + a] = 1.0
    for hd in range(heads):
        for a in range(3):
            selq[a * LANES + hd, hd * LANES + 6 + a] = 1.0
            oneq[0, hd * LANES + 3 * (hd % 2) + a] = 1.0
    faw = heads * HEAD_DIM
    col = np.concatenate([np.arange(faw) // HEAD_DIM, LANES + np.arange(faw) // HEAD_DIM])
    hsel = col[:, None] == np.arange(2 * LANES)[None, :]
    return (jnp.asarray(selk, BF16), jnp.asarray(onek), jnp.asarray(selq, BF16), jnp.asarray(oneq),
            jnp.asarray(hsel, BF16))


def _inproj(h, g, w, bf, cos, sin, lng, lnb, kbuf, vbuf, *, layer, offs, dims, seq_len, prompt):
    n, d = h.shape
    tm = TOKEN_TILE if n % TOKEN_TILE == 0 else n
    nt = n // tm
    tiles_per_seq = max(seq_len // tm, 1)
    tab_blocks = cos.shape[0] // tm
    faw, sw, rw, nh = dims["faw"], dims["sw"], dims["rw"], dims["heads"]
    npairs = nh // 2
    tile = lambda width: pl.BlockSpec((tm, width), lambda i: (i, 0))
    tab = pl.BlockSpec((tm, rw), lambda i: (i % tab_blocks, 0))
    hbm = pl.BlockSpec(memory_space=pl.ANY)
    in_specs = [tile(d), _full((1, d)), _resident(w.shape), _full((1, LANES)), tab, tab,
                _full((1, sw)), _full((1, sw)), hbm, hbm]
    args = [h, g, w, bf, cos, sin, lng, lnb, kbuf, vbuf]
    layer_rows = pl.BlockSpec((1, tm, faw), lambda i: (layer, i, 0))
    out_shape = [jax.ShapeDtypeStruct(kbuf.shape, F32), jax.ShapeDtypeStruct(vbuf.shape, F32)]
    out_specs = [layer_rows, layer_rows]
    bf_rows = jax.ShapeDtypeStruct((n, faw), BF16)
    if prompt:
        assert tm == ATTN_TILE
        sels = _bias_selectors(nh)
        in_specs += [_full(s.shape) for s in sels]
        args += list(sels)
        out_shape += [jax.ShapeDtypeStruct((nh, n, 2 * LANES), BF16),
                      jax.ShapeDtypeStruct((npairs, n, 2 * LANES), BF16),
                      jax.ShapeDtypeStruct((nh, nt, LANES, tm), BF16),
                      jax.ShapeDtypeStruct((nt, 8, LANES), F32)]
        out_specs += [pl.BlockSpec((nh, tm, 2 * LANES), lambda i: (0, i, 0)),
                      pl.BlockSpec((npairs, tm, 2 * LANES), lambda i: (0, i, 0)),
                      pl.BlockSpec((nh, 1, LANES, tm), lambda i: (0, i, 0, 0)),
                      pl.BlockSpec((1, 8, LANES), lambda i: (i, 0, 0))]
    else:
        out_shape += [bf_rows, bf_rows, bf_rows]
        out_specs += [tile(faw)] * 3
    out_shape += [jax.ShapeDtypeStruct((n, nh), F32),
                  jax.ShapeDtypeStruct((n, sw), F32), jax.ShapeDtypeStruct((n, sw), F32)]
    out_specs += [tile(nh), tile(sw), tile(sw)]
    out_shape += [jax.ShapeDtypeStruct((n, rw), BF16)] * 4
    out_specs += [tile(rw)] * 4
    scratch = [pltpu.VMEM((1, LANES), F32)] if prompt else []
    return pl.pallas_call(
        functools.partial(_inproj_kernel, offs=offs, tiles_per_seq=tiles_per_seq,
                          prompt=prompt, tm=tm, heads=nh),
        grid=(nt,), in_specs=in_specs, out_specs=out_specs, out_shape=out_shape,
        scratch_shapes=scratch, input_output_aliases={8: 0, 9: 1},
        compiler_params=_cparams(("arbitrary",)),
        name="inproj",
    )(*args)


def _fox_prompt_kernel(first_ref, fast_ref, q_ref, k_ref, vt_ref, o_ref, m_ref, acc_ref, *, t):
    qi = pl.program_id(2)
    idx = (pl.program_id(0) * pl.num_programs(1) + pl.program_id(1)) * pl.num_programs(2) + qi
    first = first_ref[idx]
    acc_ref[...] = jnp.zeros(acc_ref.shape, F32)

    def scores(ki, hh, masked):
        start = pl.multiple_of(ki * t, t)
        s = _nt_dot(k_ref[0, pl.ds(start, t), :], q_ref[hh])
        if masked:
            key = lax.broadcasted_iota(jnp.int32, (t, t), 0)
            qry = lax.broadcasted_iota(jnp.int32, (t, t), 1)
            s = jnp.where(key <= qry, s, NEG_INF)
        return s

    def plain_step(ki, masked):
        for hh in range(2):
            pr = jnp.exp2(scores(ki, hh, masked)).astype(BF16)
            acc_ref[hh] += jnp.dot(vt_ref[hh, ki], pr, preferred_element_type=F32)

    def online_step(ki, masked):
        for hh in range(2):
            s = scores(ki, hh, masked)
            m_prev = m_ref[hh]
            m_new = jnp.maximum(m_prev, jnp.max(s, axis=0, keepdims=True))
            pr = jnp.exp2(s - m_new).astype(BF16)
            acc_ref[hh] = (jnp.exp2(m_prev - m_new) * acc_ref[hh]
                           + jnp.dot(vt_ref[hh, ki], pr, preferred_element_type=F32))
            m_ref[hh] = m_new

    def run(step):
        def body(ki, carry):
            step(ki, False)
            return carry

        lax.fori_loop(first, qi, body, 0)
        step(qi, True)

    bounded = fast_ref[idx] == 1

    @pl.when(bounded)
    def _():
        run(plain_step)

    @pl.when(jnp.logical_not(bounded))
    def _():
        m_ref[...] = jnp.full(m_ref.shape, NEG_INF, F32)
        run(online_step)

    acc_a = acc_ref[0]
    acc_b = acc_ref[1]
    head_a = lax.broadcasted_iota(jnp.int32, (LANES, 1), 0) < HEAD_DIM
    o_t = jnp.where(head_a, acc_a / acc_a[HEAD_DIM:HEAD_DIM + 1, :], acc_b / acc_b[0:1, :])
    o_ref[...] = o_t.T.astype(BF16)


F32_EXP_ZERO = -104.0
PLAIN_SCORE_LIMIT = 30.0


def _attention_plan(stats, *, batch, nq, heads):
    st = stats.reshape(batch, nq, 8, LANES)[..., :heads]
    grow = 1.0 + 2.0 ** -6
    qn = jnp.sqrt(st[:, :, 0]) * grow
    kn = jnp.sqrt(jnp.max(st[:, :, 1], axis=1, keepdims=True)) * grow
    u = qn * kn
    plain = jnp.all((u <= PLAIN_SCORE_LIMIT).reshape(batch, nq, heads // 2, 2), axis=-1)
    plain_h = jnp.repeat(plain, 2, axis=-1)
    reach = jnp.where(plain_h, u, 2.0 * u)
    f_first, f_last = st[:, :, 2], st[:, :, 3]
    bound = reach[:, :, None, :] + f_first[:, :, None, :] - f_last[:, None, :, :] + 1.0
    past = (jnp.arange(nq)[None, :] < jnp.arange(nq)[:, None])[None, :, :, None]
    dead = jnp.argmax(~(past & (bound < F32_EXP_ZERO)), axis=2).astype(jnp.int32)
    dead = jnp.min(dead.reshape(batch, nq, heads // 2, 2), axis=-1)
    flat = lambda a: a.transpose(0, 2, 1).reshape(-1).astype(jnp.int32)
    return flat(dead), flat(plain)


def _fox_prompt(qa, ka, vt, stats, *, batch, seq, heads):
    n = qa.shape[1]
    t = ATTN_TILE
    nq = seq // t
    npairs = heads // 2
    first, plain = _attention_plan(stats, batch=batch, nq=nq, heads=heads)
    grid_spec = pltpu.PrefetchScalarGridSpec(
        num_scalar_prefetch=2,
        grid=(batch, npairs, nq),
        in_specs=[pl.BlockSpec((2, t, 2 * LANES), lambda b, p, i, *_: (p, b * nq + i, 0)),
                  pl.BlockSpec((1, seq, 2 * LANES), lambda b, p, i, *_: (p, b, 0)),
                  pl.BlockSpec((2, nq, LANES, t), lambda b, p, i, *_: (p, b, 0, 0))],
        out_specs=pl.BlockSpec((t, LANES), lambda b, p, i, *_: (b * nq + i, p)),
        scratch_shapes=[pltpu.VMEM((2, 1, t), F32), pltpu.VMEM((2, LANES, t), F32)],
    )
    return pl.pallas_call(
        functools.partial(_fox_prompt_kernel, t=t),
        grid_spec=grid_spec,
        out_shape=jax.ShapeDtypeStruct((n, heads * HEAD_DIM), BF16),
        compiler_params=_cparams(("arbitrary", "arbitrary", "arbitrary")),
        name="fox_prompt",
    )(first, plain, qa, ka, vt)


def _fox_sample_kernel(q_ref, kn_ref, vn_ref, pk_ref, pv_ref, plft_ref, lf_ref, lft_ref, o_ref,
                       *, heads, ts, past):
    hi = lax.Precision.HIGHEST
    r_ = lax.broadcasted_iota(jnp.int32, (past, past), 0)
    c_ = lax.broadcasted_iota(jnp.int32, (past, past), 1)
    after = jnp.where(r_ > c_, 1.0, 0.0).astype(F32)
    g_row = jnp.dot(plft_ref[0], after, precision=hi, preferred_element_type=F32)
    rr = lax.broadcasted_iota(jnp.int32, (LANES, LANES), 0)
    cc = lax.broadcasted_iota(jnp.int32, (LANES, LANES), 1)
    tril = jnp.where(cc <= rr, 1.0, 0.0).astype(F32)
    triu = jnp.where(rr <= cc, 1.0, 0.0).astype(F32)
    c_col = jnp.dot(tril, lf_ref[0], precision=hi, preferred_element_type=F32)[:ts]
    c_row = jnp.dot(lft_ref[0], triu, precision=hi, preferred_element_type=F32)
    visible = (lax.broadcasted_iota(jnp.int32, (ts, LANES), 1)
               <= lax.broadcasted_iota(jnp.int32, (ts, LANES), 0))
    lane = lax.broadcasted_iota(jnp.int32, (1, LANES), 1)
    lo = lane < HEAD_DIM
    for p in range(heads // 2):
        sl = slice(p * LANES, (p + 1) * LANES)
        q2 = q_ref[:, sl]
        kp2 = pk_ref[0, :, sl].astype(BF16)
        vp2 = pv_ref[0, :, sl].astype(BF16)
        kn2 = kn_ref[:, sl]
        vn2 = vn_ref[:, sl]
        zero = jnp.zeros_like(q2)
        res = []
        for hh in range(2):
            h = 2 * p + hh
            qh = jnp.where(lo, q2, zero) if hh == 0 else jnp.where(lo, zero, q2)
            s_p = _nt_dot(qh, kp2) + c_col[:, h:h + 1] + g_row[h:h + 1, :]
            s_n = _nt_dot(qh, kn2) + c_col[:, h:h + 1] - c_row[h:h + 1, :]
            s_n = jnp.where(visible, s_n, NEG_INF)
            m = jnp.maximum(jnp.max(s_p, axis=1, keepdims=True), jnp.max(s_n, axis=1, keepdims=True))
            pp = jnp.exp(s_p - m)
            pn = jnp.exp(s_n - m)
            l = jnp.sum(pp, axis=1, keepdims=True) + jnp.sum(pn, axis=1, keepdims=True)
            o = (jnp.dot(pp.astype(BF16), vp2, preferred_element_type=F32)
                 + jnp.dot(pn.astype(BF16), vn2, preferred_element_type=F32))
            res.append(o / l)
        o_ref[:, sl] = jnp.where(lo, res[0], res[1]).astype(BF16)


def _fox_sample(qb, kb, vb, lf, past_k, past_v, past_lf, *, batch, ts, heads):
    n, faw = qb.shape
    past = past_k.shape[1]
    plft = past_lf.transpose(0, 2, 1)
    lf3 = lf.reshape(batch, ts, heads)
    lfp = jnp.pad(lf3, ((0, 0), (0, LANES - ts), (0, LANES - heads)))
    lftp = jnp.pad(lf3.transpose(0, 2, 1), ((0, 0), (0, 0), (0, LANES - ts)))
    padr = lambda a: jnp.pad(a.reshape(batch, ts, faw), ((0, 0), (0, LANES - ts), (0, 0))
                             ).reshape(batch * LANES, faw)
    row = lambda rows: pl.BlockSpec((rows, faw), lambda b: (b, 0))
    return pl.pallas_call(
        functools.partial(_fox_sample_kernel, heads=heads, ts=ts, past=past),
        grid=(batch,),
        in_specs=[row(ts), row(LANES), row(LANES),
                  pl.BlockSpec((1, past, faw), lambda b: (b, 0, 0)),
                  pl.BlockSpec((1, past, faw), lambda b: (b, 0, 0)),
                  pl.BlockSpec((1, heads, past), lambda b: (b, 0, 0)),
                  pl.BlockSpec((1, LANES, LANES), lambda b: (b, 0, 0)),
                  pl.BlockSpec((1, heads, LANES), lambda b: (b, 0, 0))],
        out_specs=row(ts),
        out_shape=jax.ShapeDtypeStruct((n, faw), BF16),
        compiler_params=_cparams(("arbitrary",)),
        name="fox_sample",
    )(qb, padr(kb), padr(vb), past_k, past_v, plft, lfp, lftp)


def _sgu_kernel(u_ref, v_ref, w_ref, b_ref, o_ref, *, chunk, chunks_per_tile, groups):
    width = u_ref.shape[-1]
    row = lax.broadcasted_iota(jnp.int32, (chunk, chunk), 0)
    col = lax.broadcasted_iota(jnp.int32, (chunk, chunk), 1)
    causal = col <= row
    grp = lax.broadcasted_iota(jnp.int32, (1, width), 1) // HEAD_DIM
    wm = [jnp.where(causal, w_ref[g], 0.0).astype(BF16) for g in range(groups)]
    bias = b_ref[...]
    for c in range(chunks_per_tile):
        sl = slice(c * chunk, (c + 1) * chunk)
        vc = v_ref[sl, :].astype(BF16)
        mixed = jnp.zeros((chunk, width), F32)
        for g in range(groups):
            mixed = jnp.where(grp == g, jnp.dot(wm[g], vc, preferred_element_type=F32), mixed)
        o_ref[sl, :] = (u_ref[sl, :] * (mixed + bias)).astype(BF16)


def _sgu(u, v, w, bias):
    n, width = u.shape
    groups, chunk, _ = w.shape
    tile = TOKEN_TILE if (n % TOKEN_TILE == 0 and TOKEN_TILE % chunk == 0) else chunk
    tok = pl.BlockSpec((tile, width), lambda i: (i, 0))
    return pl.pallas_call(
        functools.partial(_sgu_kernel, chunk=chunk, chunks_per_tile=tile // chunk, groups=groups),
        grid=(n // tile,),
        in_specs=[tok, tok, _full(w.shape), _full(bias.shape)],
        out_specs=tok,
        out_shape=jax.ShapeDtypeStruct((n, width), BF16),
        compiler_params=_cparams(("arbitrary",)),
        name="sgu",
    )(u, v, w, bias)


def _retention_kernel(q_ref, k_ref, v_ref, g_ref, s0_ref, dmat_ref, qdec_ref, kdec_ref, sdec_ref,
                      bmask_ref, avg_ref, gn_ref, o_ref, sfin_ref, s_ref, *, npairs):
    c = pl.program_id(1)

    @pl.when(c == 0)
    def _():
        s_ref[...] = s0_ref[0]

    lane = lax.broadcasted_iota(jnp.int32, (1, LANES), 1)
    lo = lane < HEAD_DIM
    avg = avg_ref[...]

    def group_mean(x):
        x1 = x.astype(BF16)
        x2 = (x - x1.astype(F32)).astype(BF16)
        return (jnp.dot(x1, avg, preferred_element_type=F32)
                + jnp.dot(x2, avg, preferred_element_type=F32))

    for p in range(npairs):
        sl = slice(p * LANES, (p + 1) * LANES)
        q2 = q_ref[:, sl]
        k2 = k_ref[:, sl]
        v2 = v_ref[:, sl]
        zero = jnp.zeros_like(q2)
        inner = []
        for hh in range(2):
            qh = jnp.where(lo, q2, zero) if hh == 0 else jnp.where(lo, zero, q2)
            sc = _nt_dot(qh, k2) * dmat_ref[2 * p + hh]
            inner.append(jnp.dot(sc.astype(BF16), v2, preferred_element_type=F32))
        state = s_ref[p]
        cross = jnp.dot(q2, state.astype(BF16), preferred_element_type=F32) * qdec_ref[:, sl]
        o = jnp.where(lo, inner[0], inner[1]) + cross
        kd_t = (k2.astype(F32) * kdec_ref[:, sl]).T.astype(BF16)
        s_ref[p] = state * sdec_ref[p] + jnp.dot(kd_t, v2, preferred_element_type=F32) * bmask_ref[...]
        mu = group_mean(o)
        d = o - mu
        var = group_mean(d * d)
        on = d * lax.rsqrt(var + GN_EPS)
        o_ref[:, sl] = (on * gn_ref[:, sl] * g_ref[:, sl].astype(F32)).astype(BF16)

    sfin_ref[0] = s_ref[...]


def _retention(rq, rk, rv, rg, s0, tabs, gn, *, batch, rows_per_seq, blk):
    n, rw = rq.shape
    npairs = rw // LANES
    nc = rows_per_seq // blk
    tok = pl.BlockSpec((blk, rw), lambda b, c: (b * nc + c, 0))
    st = pl.BlockSpec((1, npairs, LANES, LANES), lambda b, c: (b, 0, 0, 0))
    dmat, qdec, kdec, sdec, bmask, avg = tabs
    return pl.pallas_call(
        functools.partial(_retention_kernel, npairs=npairs),
        grid=(batch, nc),
        in_specs=[tok, tok, tok, tok, st, _full(dmat.shape), _full(qdec.shape), _full(kdec.shape),
                  _full(sdec.shape), _full(bmask.shape), _full(avg.shape), _full(gn.shape)],
        out_specs=[tok, st],
        out_shape=[jax.ShapeDtypeStruct((n, rw), BF16),
                   jax.ShapeDtypeStruct((batch, npairs, LANES, LANES), F32)],
        scratch_shapes=[pltpu.VMEM((npairs, LANES, LANES), F32)],
        compiler_params=_cparams(("arbitrary", "arbitrary")),
        name="retention",
    )(rq, rk, rv, rg, s0, dmat, qdec, kdec, sdec, bmask, avg, gn)


def _retention_tables(ret_heads, true_len, blk):
    log_g = jnp.log1p(-jnp.exp(jnp.linspace(math.log(RET_DECAY_MIN), math.log(RET_DECAY_MAX),
                                            ret_heads, dtype=F32)))
    n = jnp.arange(blk, dtype=F32)
    diff = n[:, None] - n[None, :]
    causal = diff >= 0
    dmat = jnp.where(causal[None], jnp.exp(jnp.where(causal, diff, 0.0)[None] * log_g[:, None, None]), 0.0)
    per_lane = jnp.repeat(log_g, HEAD_DIM)
    qdec = jnp.exp((n[:, None] + 1.0) * per_lane[None, :])
    kdec = jnp.exp((true_len - 1.0 - n)[:, None] * per_lane[None, :])
    npairs = ret_heads // 2
    lane_head = jnp.arange(LANES) // HEAD_DIM
    bmask = (lane_head[:, None] == lane_head[None, :]).astype(F32)
    sdec = jnp.exp(true_len * per_lane).reshape(npairs, LANES)[:, :, None] * bmask[None]
    avg = (bmask / HEAD_DIM).astype(BF16)
    return dmat, qdec, kdec, sdec, bmask, avg


def _outproj_kernel(*refs, with_router, n_experts, faw, sw):
    if with_router:
        (h_ref, oa_ref, ob_ref, oc_ref, w_ref, g_ref, wr_ref, br_ref, hnew_ref, hn_ref, eg_ref) = refs
    else:
        (h_ref, oa_ref, ob_ref, oc_ref, w_ref, g_ref, hnew_ref, hn_ref) = refs
    o = (jnp.dot(oa_ref[...], w_ref[:faw, :], preferred_element_type=F32)
         + jnp.dot(ob_ref[...], w_ref[faw:faw + sw, :], preferred_element_type=F32)
         + jnp.dot(oc_ref[...], w_ref[faw + sw:, :], preferred_element_type=F32))
    h = h_ref[...] + o
    hnew_ref[...] = h
    hn = _rms(h, g_ref[...])
    hn_ref[...] = hn.astype(BF16)
    if with_router:
        x1 = hn.astype(BF16)
        x2 = (hn - x1.astype(F32)).astype(BF16)
        a = jnp.dot(x1, wr_ref[...], preferred_element_type=F32)
        b = jnp.dot(x2, wr_ref[:, :LANES], preferred_element_type=F32)
        logits = a[:, :LANES] + a[:, LANES:] + b + br_ref[...]
        lane = lax.broadcasted_iota(jnp.int32, logits.shape, 1)
        lanef = lane.astype(F32)
        logits = jnp.where(lane < n_experts, logits, -jnp.inf)
        m1 = jnp.max(logits, axis=1, keepdims=True)
        i1 = jnp.min(jnp.where(logits == m1, lanef, float(LANES)), axis=1, keepdims=True)
        rest = jnp.where(lanef == i1, -jnp.inf, logits)
        m2 = jnp.max(rest, axis=1, keepdims=True)
        i2 = jnp.min(jnp.where(rest == m2, lanef, float(LANES)), axis=1, keepdims=True)
        e2 = jnp.exp(m2 - m1)
        g1 = 1.0 / (1.0 + e2)
        g2 = e2 / (1.0 + e2)
        eg_ref[...] = jnp.where(lane == 0, g1, jnp.where(lane == 1, g2,
                                jnp.where(lane == 2, i1, jnp.where(lane == 3, i2, 0.0))))


def _outproj(h, oa, ob, oc, w, g, router=None):
    n, d = h.shape
    tm = TOKEN_TILE if n % TOKEN_TILE == 0 else n
    faw, sw, rw = oa.shape[1], ob.shape[1], oc.shape[1]
    tile = lambda width: pl.BlockSpec((tm, width), lambda i: (i, 0))
    in_specs = [tile(d), tile(faw), tile(sw), tile(rw), _resident(w.shape), _full((1, d))]
    args = [h, oa, ob, oc, w, g]
    out_shape = [jax.ShapeDtypeStruct((n, d), F32), jax.ShapeDtypeStruct((n, d), BF16)]
    out_specs = [tile(d), tile(d)]
    n_experts = 0
    if router is not None:
        wr, br, n_experts = router
        in_specs += [_full(wr.shape), _full(br.shape)]
        args += [wr, br]
        out_shape.append(jax.ShapeDtypeStruct((n, LANES), F32))
        out_specs.append(tile(LANES))
    return pl.pallas_call(
        functools.partial(_outproj_kernel, with_router=router is not None, n_experts=n_experts,
                          faw=faw, sw=sw),
        grid=(n // tm,), in_specs=in_specs, out_specs=out_specs, out_shape=out_shape,
        compiler_params=_cparams(("arbitrary",)),
        name="outproj",
    )(*args)


def _dense_ffn_kernel(h_ref, x_ref, wg_ref, wu_ref, wd_ref, o_ref, acc_ref, *, fc, nchunks):
    x = x_ref[...]
    for c in range(nchunks):
        sl = slice(c * fc, (c + 1) * fc)
        g = jnp.dot(x, wg_ref[:, sl], preferred_element_type=F32)
        u = jnp.dot(x, wu_ref[:, sl], preferred_element_type=F32)
        a = (_silu(g) * u).astype(BF16)
        y = jnp.dot(a, wd_ref[sl, :], preferred_element_type=F32)
        if c == 0:
            acc_ref[...] = y
        else:
            acc_ref[...] += y
    o_ref[...] = h_ref[...] + acc_ref[...]


def _dense_ffn(h, hn, wg, wu, wd):
    n, d = h.shape
    ff = wg.shape[1]
    tm = TOKEN_TILE if n % TOKEN_TILE == 0 else n
    fc = 256 if ff % 256 == 0 else ff
    tile = pl.BlockSpec((tm, d), lambda i: (i, 0))
    return pl.pallas_call(
        functools.partial(_dense_ffn_kernel, fc=fc, nchunks=ff // fc),
        grid=(n // tm,),
        in_specs=[tile, tile, _resident(wg.shape), _resident(wu.shape), _resident(wd.shape)],
        out_specs=tile,
        out_shape=jax.ShapeDtypeStruct((n, d), F32),
        scratch_shapes=[pltpu.VMEM((tm, d), F32)],
        compiler_params=_cparams(("arbitrary",)),
        name="dense_ffn",
    )(h, hn, wg, wu, wd)


def _moe_ffn_kernel(be_ref, nu_ref, x_ref, gate_ref, wg_ref, wu_ref, wd_ref, o_ref, acc_ref, *, nch, subs):
    b = pl.program_id(0)
    c = pl.program_id(1)

    @pl.when(b < nu_ref[0])
    def _():
        @pl.when(c == 0)
        def _():
            acc_ref[...] = jnp.zeros_like(acc_ref)

        x = x_ref[...]
        for a0, a1 in subs:
            g = jnp.dot(x, wg_ref[0, :, a0:a1], preferred_element_type=F32)
            u = jnp.dot(x, wu_ref[0, :, a0:a1], preferred_element_type=F32)
            a = (_silu(g) * u).astype(BF16)
            acc_ref[...] += jnp.dot(a, wd_ref[0, a0:a1, :], preferred_element_type=F32)

        @pl.when(c == nch - 1)
        def _():
            o_ref[...] = (acc_ref[...] * gate_ref[...]).astype(BF16)


MXU_DEPTH = 256


def _moe_ffn(xs, row_gate, block_expert, n_used, wg, wu, wd, *, rb):
    r, d = xs.shape
    ff = wg.shape[2]
    nblk = r // rb
    nch = 2 if ff % (2 * MXU_DEPTH) == 0 else 1
    fc = ff // nch
    step = 2 * MXU_DEPTH
    subs = tuple((a0, min(a0 + step, fc)) for a0 in range(0, fc, step))

    def live(b, nu):
        return jnp.minimum(b, nu[0] - 1)

    def chunk(b, c, nu):
        return jnp.where(b < nu[0], c, nch - 1)

    grid_spec = pltpu.PrefetchScalarGridSpec(
        num_scalar_prefetch=2,
        grid=(nblk, nch),
        in_specs=[pl.BlockSpec((rb, d), lambda b, c, be, nu: (live(b, nu), 0)),
                  pl.BlockSpec((rb, 1), lambda b, c, be, nu: (live(b, nu), 0)),
                  pl.BlockSpec((1, d, fc), lambda b, c, be, nu: (be[live(b, nu)], 0, chunk(b, c, nu))),
                  pl.BlockSpec((1, d, fc), lambda b, c, be, nu: (be[live(b, nu)], 0, chunk(b, c, nu))),
                  pl.BlockSpec((1, fc, d), lambda b, c, be, nu: (be[live(b, nu)], chunk(b, c, nu), 0))],
        out_specs=pl.BlockSpec((rb, d), lambda b, c, be, nu: (live(b, nu), 0)),
        scratch_shapes=[pltpu.VMEM((rb, d), F32)],
    )
    return pl.pallas_call(
        functools.partial(_moe_ffn_kernel, nch=nch, subs=subs),
        grid_spec=grid_spec,
        out_shape=jax.ShapeDtypeStruct((r, d), BF16),
        compiler_params=_cparams(("arbitrary", "arbitrary"), MOE_VMEM_LIMIT),
        name="moe_ffn",
    )(block_expert, n_used, xs, row_gate, wg, wu, wd)


def _gather_rows(x, idx):
    return x.at[idx].get(mode="promise_in_bounds")


def _moe(hn, eg, wg, wu, wd, *, rb):
    n, d = hn.shape
    n_experts = wg.shape[0]
    rows = TOP_K * n
    gates = eg[:, :TOP_K].reshape(-1)
    flat_e = eg[:, TOP_K:2 * TOP_K].astype(jnp.int32).reshape(-1)
    onehot = (flat_e[:, None] == jnp.arange(n_experts, dtype=jnp.int32)[None, :]).astype(jnp.int32)
    rank = jnp.cumsum(onehot, axis=0) - onehot
    count = jnp.sum(onehot, axis=0)
    first_sorted = jnp.cumsum(count) - count
    padded = ((count + rb - 1) // rb) * rb
    ends = jnp.cumsum(padded)
    offset = ends - padded
    pos = offset[flat_e] + jnp.take_along_axis(rank, flat_e[:, None], axis=1)[:, 0]
    nblk = pl.cdiv(rows, rb) + n_experts
    r_tot = nblk * rb
    block_start = jnp.arange(nblk, dtype=jnp.int32) * rb
    block_expert = jnp.minimum(jnp.sum((block_start[:, None] >= ends[None, :]).astype(jnp.int32), axis=1),
                               n_experts - 1).astype(jnp.int32)
    n_used = (ends[-1:] // rb).astype(jnp.int32)
    order = jnp.argsort(flat_e, stable=True).astype(jnp.int32)
    slot_e = jnp.repeat(block_expert, rb)
    local = jnp.arange(r_tot, dtype=jnp.int32) - offset[slot_e]
    valid = (local >= 0) & (local < count[slot_e])
    src_row = order[jnp.clip(first_sorted[slot_e] + local, 0, rows - 1)]
    src_tok = jnp.where(valid, src_row // TOP_K, 0)
    row_gate = jnp.where(valid, gates[src_row], 0.0)
    xs = _gather_rows(hn, src_tok)
    ys = _moe_ffn(xs, row_gate[:, None], block_expert, n_used, wg, wu, wd, rb=rb)
    return _gather_rows(ys, pos).reshape(n, TOP_K * d)


def _ple_kernel(*refs, with_moe, final, d):
    refs = list(refs)
    h_ref = refs.pop(0)
    y_ref = refs.pop(0) if with_moe else None
    p_ref, g_ref, wg_ref, wp_ref = refs[:4]
    refs = refs[4:]
    gf_ref = refs.pop(0) if final else None
    o_ref = refs[0]
    h = h_ref[...]
    if with_moe:
        h = h + y_ref[:, :d].astype(F32) + y_ref[:, d:].astype(F32)
    gate = jax.nn.sigmoid(jnp.dot(_rms(h, g_ref[...]).astype(BF16), wg_ref[...],
                                  preferred_element_type=F32))
    h = h + gate * jnp.dot(p_ref[...].astype(BF16), wp_ref[...], preferred_element_type=F32)
    if final:
        h = _rms(h, gf_ref[...])
    o_ref[...] = h


def _ple(h, y, p, g, wg, wp, g_final=None):
    n, d = h.shape
    tm = TOKEN_TILE if n % TOKEN_TILE == 0 else n
    tile = lambda width: pl.BlockSpec((tm, width), lambda i: (i, 0))
    in_specs, args = [tile(d)], [h]
    if y is not None:
        in_specs.append(tile(y.shape[1]))
        args.append(y)
    in_specs += [tile(p.shape[1]), _full((1, d)), _resident(wg.shape), _resident(wp.shape)]
    args += [p, g, wg, wp]
    if g_final is not None:
        in_specs.append(_full((1, d)))
        args.append(g_final)
    return pl.pallas_call(
        functools.partial(_ple_kernel, with_moe=y is not None, final=g_final is not None, d=d),
        grid=(n // tm,), in_specs=in_specs, out_specs=tile(d),
        out_shape=jax.ShapeDtypeStruct((n, d), F32),
        compiler_params=_cparams(("arbitrary",)),
        name="ple",
    )(*args)


def _prep_w_in(w_in, dims):
    faw, sw, rw, nh = dims["faw"], dims["sw"], dims["rw"], dims["heads"]
    sizes = (faw, faw, faw, nh, sw, sw, rw, rw, rw, rw)
    parts, o = [], 0
    for s in sizes:
        parts.append(w_in[..., o:o + s])
        o += s
    fa_q, fa_k, fa_v, fa_f, sgu_u, sgu_v, ret_q, ret_k, ret_v, ret_g = parts

    def rot_cols(w):
        lead = w.shape[:-1]
        wh = w.reshape(lead + (rw // HEAD_DIM, 2, HEAD_DIM // 2))
        return jnp.stack([-wh[..., 1, :], wh[..., 0, :]], axis=-2).reshape(lead + (rw,))

    f_pad = jnp.pad(fa_f, [(0, 0)] * (w_in.ndim - 1) + [(0, LANES - nh)])
    cols = [("q", fa_q), ("k", fa_k), ("v", fa_v), ("u", sgu_u), ("sv", sgu_v), ("rq", ret_q),
            ("rqr", rot_cols(ret_q)), ("rk", ret_k), ("rkr", rot_cols(ret_k)), ("rv", ret_v),
            ("rg", ret_g), ("f", f_pad)]
    offs, o = {}, 0
    for name, c in cols:
        offs[name] = (o, o + c.shape[-1])
        o += c.shape[-1]
    return jnp.concatenate([c for _, c in cols], axis=-1).astype(BF16), offs


def _rope_tables(pos, ret_heads):
    half = HEAD_DIM // 2
    inv = ROPE_BASE ** (-jnp.arange(half, dtype=F32) / half)
    ang = pos.astype(F32)[:, None] * inv[None, :]
    cos = jnp.tile(jnp.cos(ang), (1, 2 * ret_heads))
    sin = jnp.tile(jnp.sin(ang), (1, 2 * ret_heads))
    return cos, sin


def _pair_states(s):
    b, hds = s.shape[:2]
    s = s.reshape(b, hds // 2, 2, HEAD_DIM, HEAD_DIM)
    z = jnp.zeros_like(s[:, :, 0])
    top = jnp.concatenate([s[:, :, 0], z], axis=-1)
    bot = jnp.concatenate([z, s[:, :, 1]], axis=-1)
    return jnp.concatenate([top, bot], axis=-2)


def _unpair_states(sp):
    a = sp[:, :, :HEAD_DIM, :HEAD_DIM]
    b = sp[:, :, HEAD_DIM:, HEAD_DIM:]
    return jnp.stack([a, b], axis=2).reshape(sp.shape[0], -1, HEAD_DIM, HEAD_DIM)


def _trunk(x, p, cache, wts, dims):
    batch, t, d = x.shape
    n = batch * t
    depth = wts["w_in"].shape[0]
    heads, ret_heads = dims["heads"], dims["ret_heads"]
    has_past = cache is not None
    offset = cache[0].shape[2] if has_past else 0
    pos = offset + jnp.arange(t, dtype=jnp.int32)
    cos, sin = _rope_tables(pos, ret_heads)
    if t < TOKEN_TILE:
        cos, sin = jnp.tile(cos, (batch, 1)), jnp.tile(sin, (batch, 1))

    if has_past:
        ret_blk, ret_rows = LANES, LANES
        tabs = _retention_tables(ret_heads, t, ret_blk)
    else:
        ret_blk = RET_CHUNK if t % RET_CHUNK == 0 else t
        ret_rows = t
        tabs = _retention_tables(ret_heads, ret_blk, ret_blk)

    sgu_full = wts["sgu_w"].shape[2]
    sgu_len = sgu_full if t >= sgu_full else t

    h = x.reshape(n, d)
    new_lf, new_ret, new_sgu = [], [], []
    kbuf = jnp.zeros((depth, n, dims["faw"]), F32)
    vbuf = jnp.zeros((depth, n, dims["faw"]), F32)
    for i in range(depth):
        outs = _inproj(h, wts["g_mix"][i], wts["w_in"][i], wts["b_forget"][i], cos, sin,
                       wts["sgu_ln_g"][i], wts["sgu_ln_b"][i], kbuf, vbuf, layer=i,
                       offs=wts["offs"], dims=dims, seq_len=t, prompt=not has_past)
        if has_past:
            kbuf, vbuf, qb, kb, vb, lf, u, sv, rq, rk, rv, rg = outs
            oa = _fox_sample(qb, kb, vb, lf, cache[0][i].reshape(batch, offset, -1),
                             cache[1][i].reshape(batch, offset, -1), cache[2][i],
                             batch=batch, ts=t, heads=heads)
        else:
            kbuf, vbuf, qa, ka, vt, stats, lf, u, sv, rq, rk, rv, rg = outs
            oa = _fox_prompt(qa, ka, vt, stats, batch=batch, seq=t, heads=heads)

        sgu_w, sgu_b = wts["sgu_w"][i], wts["sgu_b"][i]
        groups = sgu_w.shape[0]
        w_l = sgu_w[:, :sgu_len, :sgu_len]
        b_l = jnp.repeat(sgu_b[:, :sgu_len].T, HEAD_DIM, axis=1)
        if sgu_len % 128 != 0:
            reps = n // sgu_len
            eye = jnp.eye(reps, dtype=F32)
            w_l = jnp.einsum("ab,gts->gatbs", eye, w_l).reshape(groups, n, n)
            b_l = jnp.tile(b_l, (reps, 1))
        ob = _sgu(u, sv, w_l, b_l)

        if has_past:
            padr = lambda a: jnp.pad(a.reshape(batch, t, -1), ((0, 0), (0, ret_rows - t), (0, 0))
                                     ).reshape(batch * ret_rows, -1)
            s0 = _pair_states(cache[3][i].astype(F32))
            oc, sfin = _retention(padr(rq), padr(rk), padr(rv), padr(rg), s0, tabs, wts["ret_gn_g"][i],
                                  batch=batch, rows_per_seq=ret_rows, blk=ret_blk)
            oc = oc.reshape(batch, ret_rows, -1)[:, :t].reshape(n, -1)
        else:
            s0 = jnp.zeros((batch, ret_heads // 2, LANES, LANES), F32)
            oc, sfin = _retention(rq, rk, rv, rg, s0, tabs, wts["ret_gn_g"][i],
                                  batch=batch, rows_per_seq=ret_rows, blk=ret_blk)

        j = i // 2
        is_moe = i % 2 == 1
        last = i == depth - 1
        g_final = wts["g_final"] if last else None
        if is_moe:
            router = (wts["w_router"][j], wts["b_router"][j], wts["n_experts"])
            h, hn, eg = _outproj(h, oa, ob, oc, wts["w_out"][i], wts["g_ffn"][i], router)
            rb = MOE_ROW_BLOCK if n >= 8 * MOE_ROW_BLOCK else LANES
            y = _moe(hn, eg, wts["w_exp_gate"][j], wts["w_exp_up"][j], wts["w_exp_down"][j], rb=rb)
            h = _ple(h, y, p[i].reshape(n, -1), wts["g_ple"][i], wts["w_ple_gate"][i],
                     wts["w_ple_proj"][i], g_final)
        else:
            h, hn = _outproj(h, oa, ob, oc, wts["w_out"][i], wts["g_ffn"][i])
            h = _dense_ffn(h, hn, wts["w_dense_gate"][j], wts["w_dense_up"][j], wts["w_dense_down"][j])
            h = _ple(h, None, p[i].reshape(n, -1), wts["g_ple"][i], wts["w_ple_gate"][i],
                     wts["w_ple_proj"][i], g_final)

        new_lf.append(lf.reshape(batch, t, heads))
        new_ret.append(_unpair_states(sfin))
        new_sgu.append(sv.reshape(batch, t, -1))

    y = h.reshape(batch, t, d)
    kv_shape = (depth, batch, t, heads, HEAD_DIM)
    return (y, kbuf.reshape(kv_shape), vbuf.reshape(kv_shape), jnp.stack(new_lf), jnp.stack(new_ret),
            jnp.stack(new_sgu))


def kernel(x_prompt, x_sample, cache_fa_k, cache_fa_v, cache_fa_logf, state_ret, p_prompt, p_sample, g_mix, w_in, b_forget, sgu_ln_g, sgu_ln_b, sgu_w, sgu_b, ret_gn_g, w_out, g_ffn, w_dense_gate, w_dense_up, w_dense_down, w_router, b_router, w_exp_gate, w_exp_up, w_exp_down, g_ple, w_ple_gate, w_ple_proj, g_final):
    heads = cache_fa_k.shape[3]
    ret_heads = state_ret.shape[2]
    groups = sgu_w.shape[1]
    n_experts = w_router.shape[-1]
    dims = dict(heads=heads, ret_heads=ret_heads, faw=heads * HEAD_DIM, sw=groups * HEAD_DIM,
                rw=ret_heads * HEAD_DIM)
    assert cache_fa_k.shape[4] == HEAD_DIM and heads % 2 == 0 and ret_heads % 2 == 0

    w_in_b, offs = _prep_w_in(w_in, dims)
    row = lambda a: a[:, None, :].astype(F32)
    wr1 = w_router.astype(BF16)
    wr2 = (w_router - wr1.astype(F32)).astype(BF16)
    pad_e = lambda a: jnp.pad(a, [(0, 0)] * (a.ndim - 1) + [(0, LANES - n_experts)])
    wts = dict(
        offs=offs, n_experts=n_experts,
        g_mix=row(g_mix), w_in=w_in_b,
        b_forget=row(jnp.pad(b_forget, ((0, 0), (0, LANES - heads)))),
        sgu_ln_g=row(sgu_ln_g), sgu_ln_b=row(sgu_ln_b), sgu_w=sgu_w, sgu_b=sgu_b,
        ret_gn_g=row(ret_gn_g), w_out=w_out.astype(BF16), g_ffn=row(g_ffn),
        w_dense_gate=w_dense_gate.astype(BF16), w_dense_up=w_dense_up.astype(BF16),
        w_dense_down=w_dense_down.astype(BF16),
        w_router=jnp.concatenate([pad_e(wr1), pad_e(wr2)], axis=-1), b_router=row(pad_e(b_router)),
        w_exp_gate=w_exp_gate.astype(BF16), w_exp_up=w_exp_up.astype(BF16),
        w_exp_down=w_exp_down.astype(BF16),
        g_ple=row(g_ple), w_ple_gate=w_ple_gate.astype(BF16), w_ple_proj=w_ple_proj.astype(BF16),
        g_final=g_final[None, :].astype(F32),
    )

    y_p, k_p, v_p, lf_p, ret_p, _ = _trunk(x_prompt, p_prompt, None, wts, dims)
    y_s, k_s, v_s, lf_s, ret_s, sgu_s = _trunk(
        x_sample, p_sample, (cache_fa_k, cache_fa_v, cache_fa_logf, state_ret), wts, dims)
    return (y_p, y_s, k_p, v_p, lf_p, ret_p, k_s, v_s, lf_s, ret_s, sgu_s)
```

```python
import functools
import math

import jax
import jax.numpy as jnp
import numpy as np
from jax import lax
from jax.experimental import pallas as pl
from jax.experimental.pallas import tpu as pltpu

F32 = jnp.float32
BF16 = jnp.bfloat16

HEAD_DIM = 64
LANES = 128
ROPE_BASE = 10000.0
RET_DECAY_MIN = 1.0 / 32.0
RET_DECAY_MAX = 1.0 / 512.0
NORM_EPS = 1e-6
GN_EPS = 1e-5
NEG_INF = -1e30
TOP_K = 2
LOG2E = 1.4426950408889634

TOKEN_TILE = 512
ATTN_TILE = 512
RET_CHUNK = 256
MOE_ROW_BLOCK = 1024
VMEM_LIMIT = 48 * 1024 * 1024
MOE_VMEM_LIMIT = 56 * 1024 * 1024


def _cparams(sem, vmem=VMEM_LIMIT):
    return pltpu.CompilerParams(dimension_semantics=sem, vmem_limit_bytes=vmem)


def _full(shape):
    zeros = (0,) * len(shape)
    return pl.BlockSpec(shape, lambda *_: zeros)


def _resident(shape):
    zeros = (0,) * len(shape)
    return pl.BlockSpec(shape, lambda *_: zeros, pipeline_mode=pl.Buffered(1))


def _rms(x, g):
    return x * lax.rsqrt(jnp.mean(x * x, axis=-1, keepdims=True) + NORM_EPS) * g


def _silu(x):
    return x * (1.0 / (1.0 + jnp.exp(-x)))


def _split3(x):
    p1 = x.astype(BF16)
    r1 = x - p1.astype(F32)
    p2 = r1.astype(BF16)
    p3 = (r1 - p2.astype(F32)).astype(BF16)
    return p1, p2, p3


def _nt_dot(a, b):
    return lax.dot_general(a, b, (((1,), (1,)), ((), ())), preferred_element_type=F32)


def _inproj_kernel(*refs, offs, tiles_per_seq, prompt, tm, heads):
    (h_ref, g_ref, w_ref, bf_ref, cos_ref, sin_ref, lng_ref, lnb_ref) = refs[:8]
    if prompt:
        (selk_ref, onek_ref, selq_ref, oneq_ref, hsel_ref) = refs[10:15]
        (k32_ref, v32_ref, qa_ref, ka_ref, vt_ref, st_ref, lf_ref, u_ref, sv_ref,
         rq_ref, rk_ref, rv_ref, rg_ref, carry_ref) = refs[15:]
    else:
        (k32_ref, v32_ref, q_ref, kb_ref, vb_ref, lf_ref, u_ref, sv_ref,
         rq_ref, rk_ref, rv_ref, rg_ref) = refs[10:]

    xn = _rms(h_ref[...], g_ref[...]).astype(BF16)

    def proj(name):
        a, b = offs[name]
        return jnp.dot(xn, w_ref[:, a:b], preferred_element_type=F32)

    q = proj("q") * (HEAD_DIM ** -0.5)
    k = proj("k")
    v = proj("v")
    for hd in range(heads):
        hs = slice(hd * HEAD_DIM, (hd + 1) * HEAD_DIM)
        k32_ref[0, pl.ds(hd, tm, stride=heads), :] = k[:, hs]
        v32_ref[0, pl.ds(hd, tm, stride=heads), :] = v[:, hs]
    if not prompt:
        q_ref[...] = q.astype(BF16)
        kb_ref[...] = k.astype(BF16)
        vb_ref[...] = v.astype(BF16)

    u_ref[...] = jax.nn.gelu(proj("u"))
    sv = jax.nn.gelu(proj("sv"))
    mu = jnp.mean(sv, axis=-1, keepdims=True)
    var = jnp.mean(jnp.square(sv - mu), axis=-1, keepdims=True)
    sv_ref[...] = (sv - mu) * lax.rsqrt(var + GN_EPS) * lng_ref[...] + lnb_ref[...]

    cos = cos_ref[...]
    sin = sin_ref[...]
    rq_ref[...] = (proj("rq") * cos + proj("rqr") * sin).astype(BF16)
    rk_ref[...] = ((proj("rk") * cos + proj("rkr") * sin) * (HEAD_DIM ** -0.5)).astype(BF16)
    rv_ref[...] = proj("rv").astype(BF16)
    rg_ref[...] = _silu(proj("rg")).astype(BF16)

    f = proj("f") + bf_ref[...]
    lf = jnp.minimum(f, 0.0) - jnp.log1p(jnp.exp(-jnp.abs(f)))
    nh = lf_ref.shape[-1]
    lf_ref[...] = lf[:, :nh]

    if prompt:
        i = pl.program_id(0)

        @pl.when(i % tiles_per_seq == 0)
        def _():
            carry_ref[...] = jnp.zeros_like(carry_ref)

        row = lax.broadcasted_iota(jnp.int32, (tm, tm), 0)
        col = lax.broadcasted_iota(jnp.int32, (tm, tm), 1)
        tril = jnp.where(col <= row, 1.0, 0.0).astype(BF16)
        pieces = jnp.concatenate(_split3(lf), axis=1)
        cs = jnp.dot(tril, pieces, preferred_element_type=F32)
        fc = cs[:, :LANES] + cs[:, LANES:2 * LANES] + cs[:, 2 * LANES:] + carry_ref[...]
        carry_ref[...] = fc[tm - 1:tm, :]

        fpieces = jnp.concatenate(_split3(fc * LOG2E), axis=1)
        kaug = jnp.dot(fpieces, selk_ref[...], preferred_element_type=F32) + onek_ref[...]
        qaug = jnp.dot(fpieces, selq_ref[...], preferred_element_type=F32) + oneq_ref[...]

        sq = jnp.concatenate([q * q, k * k], axis=1) * (1.0 + 2.0 ** -7)
        n2 = jnp.max(jnp.dot(sq.astype(BF16), hsel_ref[...], preferred_element_type=F32),
                     axis=0, keepdims=True)
        st_ref[0] = jnp.concatenate([n2[:, :LANES], n2[:, LANES:], fc[0:1, :], fc[tm - 1:tm, :],
                                     jnp.zeros((4, LANES), F32)], axis=0)

        lane_head = lax.broadcasted_iota(jnp.int32, (1, LANES), 1) // HEAD_DIM
        q2 = q * LOG2E
        for p in range(heads // 2):
            sl = slice(p * LANES, (p + 1) * LANES)
            ka_ref[p, :, :LANES] = k[:, sl].astype(BF16)
            ka_ref[p, :, LANES:] = kaug[:, sl].astype(BF16)
            for hh in range(2):
                hd = 2 * p + hh
                own = lane_head == hh
                qa_ref[hd, :, :LANES] = jnp.where(own, q2[:, sl], 0.0).astype(BF16)
                qa_ref[hd, :, LANES:] = qaug[:, hd * LANES:(hd + 1) * LANES].astype(BF16)
                vt_ref[hd, 0] = jnp.where(own, v[:, sl], 1.0).T.astype(BF16)


def _bias_selectors(heads):
    npairs = heads // 2
    selk = np.zeros((3 * LANES, npairs * LANES), np.float32)
    onek = np.zeros((1, npairs * LANES), np.float32)
    selq = np.zeros((3 * LANES, heads * LANES), np.float32)
    oneq = np.zeros((1, heads * LANES), np.float32)
    for p in range(npairs):
        for a in range(3):
            for hh in range(2):
                selk[a * LANES + 2 * p + hh, p * LANES + 3 * hh + a] = -1.0
            onek[0, p * LANES + 6 + a] = 1.0
    for hd in range(heads):
        for a in range(3):
            selq[a * LANES + hd, hd * LANES + 6 + a] = 1.0
            oneq[0, hd * LANES + 3 * (hd % 2) + a] = 1.0
    faw = heads * HEAD_DIM
    col = np.concatenate([np.arange(faw) // HEAD_DIM, LANES + np.arange(faw) // HEAD_DIM])
    hsel = col[:, None] == np.arange(2 * LANES)[None, :]
    return (jnp.asarray(selk, BF16), jnp.asarray(onek), jnp.asarray(selq, BF16), jnp.asarray(oneq),
            jnp.asarray(hsel, BF16))


def _inproj(h, g, w, bf, cos, sin, lng, lnb, kbuf, vbuf, *, layer, offs, dims, seq_len, prompt):
    n, d = h.shape
    tm = TOKEN_TILE if n % TOKEN_TILE == 0 else n
    nt = n // tm
    tiles_per_seq = max(seq_len // tm, 1)
    tab_blocks = cos.shape[0] // tm
    faw, sw, rw, nh = dims["faw"], dims["sw"], dims["rw"], dims["heads"]
    npairs = nh // 2
    tile = lambda width: pl.BlockSpec((tm, width), lambda i: (i, 0))
    tab = pl.BlockSpec((tm, rw), lambda i: (i % tab_blocks, 0))
    hbm = pl.BlockSpec(memory_space=pl.ANY)
    in_specs = [tile(d), _full((1, d)), _resident(w.shape), _full((1, LANES)), tab, tab,
                _full((1, sw)), _full((1, sw)), hbm, hbm]
    args = [h, g, w, bf, cos, sin, lng, lnb, kbuf, vbuf]
    layer_rows = pl.BlockSpec((1, tm * nh, HEAD_DIM), lambda i: (layer, i, 0))
    out_shape = [jax.ShapeDtypeStruct(kbuf.shape, F32), jax.ShapeDtypeStruct(vbuf.shape, F32)]
    out_specs = [layer_rows, layer_rows]
    bf_rows = jax.ShapeDtypeStruct((n, faw), BF16)
    if prompt:
        assert tm == ATTN_TILE
        sels = _bias_selectors(nh)
        in_specs += [_full(s.shape) for s in sels]
        args += list(sels)
        out_shape += [jax.ShapeDtypeStruct((nh, n, 2 * LANES), BF16),
                      jax.ShapeDtypeStruct((npairs, n, 2 * LANES), BF16),
                      jax.ShapeDtypeStruct((nh, nt, LANES, tm), BF16),
                      jax.ShapeDtypeStruct((nt, 8, LANES), F32)]
        out_specs += [pl.BlockSpec((nh, tm, 2 * LANES), lambda i: (0, i, 0)),
                      pl.BlockSpec((npairs, tm, 2 * LANES), lambda i: (0, i, 0)),
                      pl.BlockSpec((nh, 1, LANES, tm), lambda i: (0, i, 0, 0)),
                      pl.BlockSpec((1, 8, LANES), lambda i: (i, 0, 0))]
    else:
        out_shape += [bf_rows, bf_rows, bf_rows]
        out_specs += [tile(faw)] * 3
    out_shape += [jax.ShapeDtypeStruct((n, nh), F32),
                  jax.ShapeDtypeStruct((n, sw), F32), jax.ShapeDtypeStruct((n, sw), F32)]
    out_specs += [tile(nh), tile(sw), tile(sw)]
    out_shape += [jax.ShapeDtypeStruct((n, rw), BF16)] * 4
    out_specs += [tile(rw)] * 4
    scratch = [pltpu.VMEM((1, LANES), F32)] if prompt else []
    return pl.pallas_call(
        functools.partial(_inproj_kernel, offs=offs, tiles_per_seq=tiles_per_seq,
                          prompt=prompt, tm=tm, heads=nh),
        grid=(nt,), in_specs=in_specs, out_specs=out_specs, out_shape=out_shape,
        scratch_shapes=scratch, input_output_aliases={8: 0, 9: 1},
        compiler_params=_cparams(("arbitrary",)),
        name="inproj",
    )(*args)


def _fox_prompt_kernel(first_ref, fast_ref, q_ref, k_ref, vt_ref, o_ref, m_ref, acc_ref, *, t):
    qi = pl.program_id(2)
    idx = (pl.program_id(0) * pl.num_programs(1) + pl.program_id(1)) * pl.num_programs(2) + qi
    first = first_ref[idx]
    acc_ref[...] = jnp.zeros(acc_ref.shape, F32)

    def scores(ki, hh, masked):
        start = pl.multiple_of(ki * t, t)
        s = _nt_dot(k_ref[0, pl.ds(start, t), :], q_ref[hh])
        if masked:
            key = lax.broadcasted_iota(jnp.int32, (t, t), 0)
            qry = lax.broadcasted_iota(jnp.int32, (t, t), 1)
            s = jnp.where(key <= qry, s, NEG_INF)
        return s

    def plain_step(ki, masked):
        pr = [jnp.exp2(scores(ki, hh, masked)).astype(BF16) for hh in range(2)]
        for hh in range(2):
            acc_ref[hh] += jnp.dot(vt_ref[hh, ki], pr[hh], preferred_element_type=F32)

    def online_step(ki, masked):
        for hh in range(2):
            s = scores(ki, hh, masked)
            m_prev = m_ref[hh]
            m_new = jnp.maximum(m_prev, jnp.max(s, axis=0, keepdims=True))
            pr = jnp.exp2(s - m_new).astype(BF16)
            acc_ref[hh] = (jnp.exp2(m_prev - m_new) * acc_ref[hh]
                           + jnp.dot(vt_ref[hh, ki], pr, preferred_element_type=F32))
            m_ref[hh] = m_new

    def run(step):
        def body(ki, carry):
            step(ki, False)
            return carry

        lax.fori_loop(first, qi, body, 0)
        step(qi, True)

    bounded = fast_ref[idx] == 1

    @pl.when(bounded)
    def _():
        run(plain_step)

    @pl.when(jnp.logical_not(bounded))
    def _():
        m_ref[...] = jnp.full(m_ref.shape, NEG_INF, F32)
        run(online_step)

    acc_a = acc_ref[0]
    acc_b = acc_ref[1]
    head_a = lax.broadcasted_iota(jnp.int32, (LANES, 1), 0) < HEAD_DIM
    o_t = jnp.where(head_a, acc_a / acc_a[HEAD_DIM:HEAD_DIM + 1, :], acc_b / acc_b[0:1, :])
    o_ref[...] = o_t.T.astype(BF16)


F32_EXP_ZERO = -104.0
PLAIN_SCORE_LIMIT = 30.0


def _attention_plan(stats, *, batch, nq, heads):
    st = stats.reshape(batch, nq, 8, LANES)[..., :heads]
    grow = 1.0 + 2.0 ** -6
    qn = jnp.sqrt(st[:, :, 0]) * grow
    kn = jnp.sqrt(jnp.max(st[:, :, 1], axis=1, keepdims=True)) * grow
    u = qn * kn
    plain = jnp.all((u <= PLAIN_SCORE_LIMIT).reshape(batch, nq, heads // 2, 2), axis=-1)
    plain_h = jnp.repeat(plain, 2, axis=-1)
    reach = jnp.where(plain_h, u, 2.0 * u)
    f_first, f_last = st[:, :, 2], st[:, :, 3]
    bound = reach[:, :, None, :] + f_first[:, :, None, :] - f_last[:, None, :, :] + 1.0
    past = (jnp.arange(nq)[None, :] < jnp.arange(nq)[:, None])[None, :, :, None]
    dead = jnp.argmax(~(past & (bound < F32_EXP_ZERO)), axis=2).astype(jnp.int32)
    dead = jnp.min(dead.reshape(batch, nq, heads // 2, 2), axis=-1)
    flat = lambda a: a.transpose(0, 2, 1).reshape(-1).astype(jnp.int32)
    return flat(dead), flat(plain)


def _fox_prompt(qa, ka, vt, stats, *, batch, seq, heads):
    n = qa.shape[1]
    t = ATTN_TILE
    nq = seq // t
    npairs = heads // 2
    first, plain = _attention_plan(stats, batch=batch, nq=nq, heads=heads)
    grid_spec = pltpu.PrefetchScalarGridSpec(
        num_scalar_prefetch=2,
        grid=(batch, npairs, nq),
        in_specs=[pl.BlockSpec((2, t, 2 * LANES), lambda b, p, i, *_: (p, b * nq + i, 0)),
                  pl.BlockSpec((1, seq, 2 * LANES), lambda b, p, i, *_: (p, b, 0)),
                  pl.BlockSpec((2, nq, LANES, t), lambda b, p, i, *_: (p, b, 0, 0))],
        out_specs=pl.BlockSpec((t, LANES), lambda b, p, i, *_: (b * nq + i, p)),
        scratch_shapes=[pltpu.VMEM((2, 1, t), F32), pltpu.VMEM((2, LANES, t), F32)],
    )
    return pl.pallas_call(
        functools.partial(_fox_prompt_kernel, t=t),
        grid_spec=grid_spec,
        out_shape=jax.ShapeDtypeStruct((n, heads * HEAD_DIM), BF16),
        compiler_params=_cparams(("arbitrary", "arbitrary", "arbitrary")),
        name="fox_prompt",
    )(first, plain, qa, ka, vt)


def _fox_sample_kernel(q_ref, kn_ref, vn_ref, pk_ref, pv_ref, plft_ref, lf_ref, lft_ref, o_ref,
                       *, heads, ts, past):
    hi = lax.Precision.HIGHEST
    r_ = lax.broadcasted_iota(jnp.int32, (past, past), 0)
    c_ = lax.broadcasted_iota(jnp.int32, (past, past), 1)
    after = jnp.where(r_ > c_, 1.0, 0.0).astype(F32)
    g_row = jnp.dot(plft_ref[0], after, precision=hi, preferred_element_type=F32)
    rr = lax.broadcasted_iota(jnp.int32, (LANES, LANES), 0)
    cc = lax.broadcasted_iota(jnp.int32, (LANES, LANES), 1)
    tril = jnp.where(cc <= rr, 1.0, 0.0).astype(F32)
    triu = jnp.where(rr <= cc, 1.0, 0.0).astype(F32)
    c_col = jnp.dot(tril, lf_ref[0], precision=hi, preferred_element_type=F32)[:ts]
    c_row = jnp.dot(lft_ref[0], triu, precision=hi, preferred_element_type=F32)
    visible = (lax.broadcasted_iota(jnp.int32, (ts, LANES), 1)
               <= lax.broadcasted_iota(jnp.int32, (ts, LANES), 0))
    lane = lax.broadcasted_iota(jnp.int32, (1, LANES), 1)
    lo = lane < HEAD_DIM
    for p in range(heads // 2):
        sl = slice(p * LANES, (p + 1) * LANES)
        q2 = q_ref[:, sl]
        kp2 = pk_ref[0, :, sl].astype(BF16)
        vp2 = pv_ref[0, :, sl].astype(BF16)
        kn2 = kn_ref[:, sl]
        vn2 = vn_ref[:, sl]
        zero = jnp.zeros_like(q2)
        res = []
        for hh in range(2):
            h = 2 * p + hh
            qh = jnp.where(lo, q2, zero) if hh == 0 else jnp.where(lo, zero, q2)
            s_p = _nt_dot(qh, kp2) + c_col[:, h:h + 1] + g_row[h:h + 1, :]
            s_n = _nt_dot(qh, kn2) + c_col[:, h:h + 1] - c_row[h:h + 1, :]
            s_n = jnp.where(visible, s_n, NEG_INF)
            m = jnp.maximum(jnp.max(s_p, axis=1, keepdims=True), jnp.max(s_n, axis=1, keepdims=True))
            pp = jnp.exp(s_p - m)
            pn = jnp.exp(s_n - m)
            l = jnp.sum(pp, axis=1, keepdims=True) + jnp.sum(pn, axis=1, keepdims=True)
            o = (jnp.dot(pp.astype(BF16), vp2, preferred_element_type=F32)
                 + jnp.dot(pn.astype(BF16), vn2, preferred_element_type=F32))
            res.append(o / l)
        o_ref[:, sl] = jnp.where(lo, res[0], res[1]).astype(BF16)


def _fox_sample(qb, kb, vb, lf, past_k, past_v, past_lf, *, batch, ts, heads):
    n, faw = qb.shape
    past = past_k.shape[1]
    plft = past_lf.transpose(0, 2, 1)
    lf3 = lf.reshape(batch, ts, heads)
    lfp = jnp.pad(lf3, ((0, 0), (0, LANES - ts), (0, LANES - heads)))
    lftp = jnp.pad(lf3.transpose(0, 2, 1), ((0, 0), (0, 0), (0, LANES - ts)))
    padr = lambda a: jnp.pad(a.reshape(batch, ts, faw), ((0, 0), (0, LANES - ts), (0, 0))
                             ).reshape(batch * LANES, faw)
    row = lambda rows: pl.BlockSpec((rows, faw), lambda b: (b, 0))
    return pl.pallas_call(
        functools.partial(_fox_sample_kernel, heads=heads, ts=ts, past=past),
        grid=(batch,),
        in_specs=[row(ts), row(LANES), row(LANES),
                  pl.BlockSpec((1, past, faw), lambda b: (b, 0, 0)),
                  pl.BlockSpec((1, past, faw), lambda b: (b, 0, 0)),
                  pl.BlockSpec((1, heads, past), lambda b: (b, 0, 0)),
                  pl.BlockSpec((1, LANES, LANES), lambda b: (b, 0, 0)),
                  pl.BlockSpec((1, heads, LANES), lambda b: (b, 0, 0))],
        out_specs=row(ts),
        out_shape=jax.ShapeDtypeStruct((n, faw), BF16),
        compiler_params=_cparams(("arbitrary",)),
        name="fox_sample",
    )(qb, padr(kb), padr(vb), past_k, past_v, plft, lfp, lftp)


def _sgu_kernel(u_ref, v_ref, w_ref, b_ref, o_ref, *, chunk, chunks_per_tile, groups):
    width = u_ref.shape[-1]
    row = lax.broadcasted_iota(jnp.int32, (chunk, chunk), 0)
    col = lax.broadcasted_iota(jnp.int32, (chunk, chunk), 1)
    causal = col <= row
    grp = lax.broadcasted_iota(jnp.int32, (1, width), 1) // HEAD_DIM
    wm = [jnp.where(causal, w_ref[g], 0.0).astype(BF16) for g in range(groups)]
    bias = b_ref[...]
    for c in range(chunks_per_tile):
        sl = slice(c * chunk, (c + 1) * chunk)
        vc = v_ref[sl, :].astype(BF16)
        mixed = jnp.zeros((chunk, width), F32)
        for g in range(groups):
            mixed = jnp.where(grp == g, jnp.dot(wm[g], vc, preferred_element_type=F32), mixed)
        o_ref[sl, :] = (u_ref[sl, :] * (mixed + bias)).astype(BF16)


def _sgu(u, v, w, bias):
    n, width = u.shape
    groups, chunk, _ = w.shape
    tile = TOKEN_TILE if (n % TOKEN_TILE == 0 and TOKEN_TILE % chunk == 0) else chunk
    tok = pl.BlockSpec((tile, width), lambda i: (i, 0))
    return pl.pallas_call(
        functools.partial(_sgu_kernel, chunk=chunk, chunks_per_tile=tile // chunk, groups=groups),
        grid=(n // tile,),
        in_specs=[tok, tok, _full(w.shape), _full(bias.shape)],
        out_specs=tok,
        out_shape=jax.ShapeDtypeStruct((n, width), BF16),
        compiler_params=_cparams(("arbitrary",)),
        name="sgu",
    )(u, v, w, bias)


def _retention_kernel(q_ref, k_ref, v_ref, g_ref, s0_ref, dmat_ref, qdec_ref, kdec_ref, sdec_ref,
                      bmask_ref, avg_ref, gn_ref, o_ref, sfin_ref, s_ref, *, npairs):
    c = pl.program_id(1)

    @pl.when(c == 0)
    def _():
        s_ref[...] = s0_ref[0]

    lane = lax.broadcasted_iota(jnp.int32, (1, LANES), 1)
    lo = lane < HEAD_DIM
    avg = avg_ref[...]

    def group_mean(x):
        x1 = x.astype(BF16)
        x2 = (x - x1.astype(F32)).astype(BF16)
        return (jnp.dot(x1, avg, preferred_element_type=F32)
                + jnp.dot(x2, avg, preferred_element_type=F32))

    for p in range(npairs):
        sl = slice(p * LANES, (p + 1) * LANES)
        q2 = q_ref[:, sl]
        k2 = k_ref[:, sl]
        v2 = v_ref[:, sl]
        zero = jnp.zeros_like(q2)
        inner = []
        for hh in range(2):
            qh = jnp.where(lo, q2, zero) if hh == 0 else jnp.where(lo, zero, q2)
            sc = _nt_dot(qh, k2) * dmat_ref[2 * p + hh]
            inner.append(jnp.dot(sc.astype(BF16), v2, preferred_element_type=F32))
        state = s_ref[p]
        cross = jnp.dot(q2, state.astype(BF16), preferred_element_type=F32) * qdec_ref[:, sl]
        o = jnp.where(lo, inner[0], inner[1]) + cross
        kd_t = (k2.astype(F32) * kdec_ref[:, sl]).T.astype(BF16)
        s_ref[p] = state * sdec_ref[p] + jnp.dot(kd_t, v2, preferred_element_type=F32) * bmask_ref[...]
        mu = group_mean(o)
        d = o - mu
        var = group_mean(d * d)
        on = d * lax.rsqrt(var + GN_EPS)
        o_ref[:, sl] = (on * gn_ref[:, sl] * g_ref[:, sl].astype(F32)).astype(BF16)

    sfin_ref[0] = s_ref[...]


def _retention(rq, rk, rv, rg, s0, tabs, gn, *, batch, rows_per_seq, blk):
    n, rw = rq.shape
    npairs = rw // LANES
    nc = rows_per_seq // blk
    tok = pl.BlockSpec((blk, rw), lambda b, c: (b * nc + c, 0))
    st = pl.BlockSpec((1, npairs, LANES, LANES), lambda b, c: (b, 0, 0, 0))
    dmat, qdec, kdec, sdec, bmask, avg = tabs
    return pl.pallas_call(
        functools.partial(_retention_kernel, npairs=npairs),
        grid=(batch, nc),
        in_specs=[tok, tok, tok, tok, st, _full(dmat.shape), _full(qdec.shape), _full(kdec.shape),
                  _full(sdec.shape), _full(bmask.shape), _full(avg.shape), _full(gn.shape)],
        out_specs=[tok, st],
        out_shape=[jax.ShapeDtypeStruct((n, rw), BF16),
                   jax.ShapeDtypeStruct((batch, npairs, LANES, LANES), F32)],
        scratch_shapes=[pltpu.VMEM((npairs, LANES, LANES), F32)],
        compiler_params=_cparams(("arbitrary", "arbitrary")),
        name="retention",
    )(rq, rk, rv, rg, s0, dmat, qdec, kdec, sdec, bmask, avg, gn)


def _retention_tables(ret_heads, true_len, blk):
    log_g = jnp.log1p(-jnp.exp(jnp.linspace(math.log(RET_DECAY_MIN), math.log(RET_DECAY_MAX),
                                            ret_heads, dtype=F32)))
    n = jnp.arange(blk, dtype=F32)
    diff = n[:, None] - n[None, :]
    causal = diff >= 0
    dmat = jnp.where(causal[None], jnp.exp(jnp.where(causal, diff, 0.0)[None] * log_g[:, None, None]), 0.0)
    per_lane = jnp.repeat(log_g, HEAD_DIM)
    qdec = jnp.exp((n[:, None] + 1.0) * per_lane[None, :])
    kdec = jnp.exp((true_len - 1.0 - n)[:, None] * per_lane[None, :])
    npairs = ret_heads // 2
    lane_head = jnp.arange(LANES) // HEAD_DIM
    bmask = (lane_head[:, None] == lane_head[None, :]).astype(F32)
    sdec = jnp.exp(true_len * per_lane).reshape(npairs, LANES)[:, :, None] * bmask[None]
    avg = (bmask / HEAD_DIM).astype(BF16)
    return dmat, qdec, kdec, sdec, bmask, avg


def _outproj_kernel(*refs, with_router, n_experts, faw, sw):
    if with_router:
        (h_ref, oa_ref, ob_ref, oc_ref, w_ref, g_ref, wr_ref, br_ref, hnew_ref, hn_ref, eg_ref) = refs
    else:
        (h_ref, oa_ref, ob_ref, oc_ref, w_ref, g_ref, hnew_ref, hn_ref) = refs
    o = (jnp.dot(oa_ref[...], w_ref[:faw, :], preferred_element_type=F32)
         + jnp.dot(ob_ref[...], w_ref[faw:faw + sw, :], preferred_element_type=F32)
         + jnp.dot(oc_ref[...], w_ref[faw + sw:, :], preferred_element_type=F32))
    h = h_ref[...] + o
    hnew_ref[...] = h
    hn = _rms(h, g_ref[...])
    hn_ref[...] = hn.astype(BF16)
    if with_router:
        x1 = hn.astype(BF16)
        x2 = (hn - x1.astype(F32)).astype(BF16)
        a = jnp.dot(x1, wr_ref[...], preferred_element_type=F32)
        b = jnp.dot(x2, wr_ref[:, :LANES], preferred_element_type=F32)
        logits = a[:, :LANES] + a[:, LANES:] + b + br_ref[...]
        lane = lax.broadcasted_iota(jnp.int32, logits.shape, 1)
        lanef = lane.astype(F32)
        logits = jnp.where(lane < n_experts, logits, -jnp.inf)
        m1 = jnp.max(logits, axis=1, keepdims=True)
        i1 = jnp.min(jnp.where(logits == m1, lanef, float(LANES)), axis=1, keepdims=True)
        rest = jnp.where(lanef == i1, -jnp.inf, logits)
        m2 = jnp.max(rest, axis=1, keepdims=True)
        i2 = jnp.min(jnp.where(rest == m2, lanef, float(LANES)), axis=1, keepdims=True)
        e2 = jnp.exp(m2 - m1)
        g1 = 1.0 / (1.0 + e2)
        g2 = e2 / (1.0 + e2)
        eg_ref[...] = jnp.where(lane == 0, g1, jnp.where(lane == 1, g2,
                                jnp.where(lane == 2, i1, jnp.where(lane == 3, i2, 0.0))))


def _outproj(h, oa, ob, oc, w, g, router=None):
    n, d = h.shape
    tm = TOKEN_TILE if n % TOKEN_TILE == 0 else n
    faw, sw, rw = oa.shape[1], ob.shape[1], oc.shape[1]
    tile = lambda width: pl.BlockSpec((tm, width), lambda i: (i, 0))
    in_specs = [tile(d), tile(faw), tile(sw), tile(rw), _resident(w.shape), _full((1, d))]
    args = [h, oa, ob, oc, w, g]
    out_shape = [jax.ShapeDtypeStruct((n, d), F32), jax.ShapeDtypeStruct((n, d), BF16)]
    out_specs = [tile(d), tile(d)]
    n_experts = 0
    if router is not None:
        wr, br, n_experts = router
        in_specs += [_full(wr.shape), _full(br.shape)]
        args += [wr, br]
        out_shape.append(jax.ShapeDtypeStruct((n, LANES), F32))
        out_specs.append(tile(LANES))
    return pl.pallas_call(
        functools.partial(_outproj_kernel, with_router=router is not None, n_experts=n_experts,
                          faw=faw, sw=sw),
        grid=(n // tm,), in_specs=in_specs, out_specs=out_specs, out_shape=out_shape,
        compiler_params=_cparams(("arbitrary",)),
        name="outproj",
    )(*args)


def _dense_ffn_kernel(h_ref, x_ref, wg_ref, wu_ref, wd_ref, o_ref, acc_ref, *, fc, nchunks):
    x = x_ref[...]
    for c in range(nchunks):
        sl = slice(c * fc, (c + 1) * fc)
        g = jnp.dot(x, wg_ref[:, sl], preferred_element_type=F32)
        u = jnp.dot(x, wu_ref[:, sl], preferred_element_type=F32)
        a = (_silu(g) * u).astype(BF16)
        y = jnp.dot(a, wd_ref[sl, :], preferred_element_type=F32)
        if c == 0:
            acc_ref[...] = y
        else:
            acc_ref[...] += y
    o_ref[...] = h_ref[...] + acc_ref[...]


def _dense_ffn(h, hn, wg, wu, wd):
    n, d = h.shape
    ff = wg.shape[1]
    tm = TOKEN_TILE if n % TOKEN_TILE == 0 else n
    fc = 256 if ff % 256 == 0 else ff
    tile = pl.BlockSpec((tm, d), lambda i: (i, 0))
    return pl.pallas_call(
        functools.partial(_dense_ffn_kernel, fc=fc, nchunks=ff // fc),
        grid=(n // tm,),
        in_specs=[tile, tile, _resident(wg.shape), _resident(wu.shape), _resident(wd.shape)],
        out_specs=tile,
        out_shape=jax.ShapeDtypeStruct((n, d), F32),
        scratch_shapes=[pltpu.VMEM((tm, d), F32)],
        compiler_params=_cparams(("arbitrary",)),
        name="dense_ffn",
    )(h, hn, wg, wu, wd)


def _moe_ffn_kernel(be_ref, nu_ref, x_ref, gate_ref, wg_ref, wu_ref, wd_ref, o_ref, acc_ref, *, nch, subs):
    b = pl.program_id(0)
    c = pl.program_id(1)

    @pl.when(b < nu_ref[0])
    def _():
        @pl.when(c == 0)
        def _():
            acc_ref[...] = jnp.zeros_like(acc_ref)

        x = x_ref[...]
        for a0, a1 in subs:
            g = jnp.dot(x, wg_ref[0, :, a0:a1], preferred_element_type=F32)
            u = jnp.dot(x, wu_ref[0, :, a0:a1], preferred_element_type=F32)
            a = (_silu(g) * u).astype(BF16)
            acc_ref[...] += jnp.dot(a, wd_ref[0, a0:a1, :], preferred_element_type=F32)

        @pl.when(c == nch - 1)
        def _():
            o_ref[...] = (acc_ref[...] * gate_ref[...]).astype(BF16)


MXU_DEPTH = 256


def _moe_ffn(xs, row_gate, block_expert, n_used, wg, wu, wd, *, rb):
    r, d = xs.shape
    ff = wg.shape[2]
    nblk = r // rb
    nch = 2 if ff % (2 * MXU_DEPTH) == 0 else 1
    fc = ff // nch
    step = 2 * MXU_DEPTH
    subs = tuple((a0, min(a0 + step, fc)) for a0 in range(0, fc, step))

    def live(b, nu):
        return jnp.minimum(b, nu[0] - 1)

    def chunk(b, c, nu):
        return jnp.where(b < nu[0], c, nch - 1)

    grid_spec = pltpu.PrefetchScalarGridSpec(
        num_scalar_prefetch=2,
        grid=(nblk, nch),
        in_specs=[pl.BlockSpec((rb, d), lambda b, c, be, nu: (live(b, nu), 0)),
                  pl.BlockSpec((rb, 1), lambda b, c, be, nu: (live(b, nu), 0)),
                  pl.BlockSpec((1, d, fc), lambda b, c, be, nu: (be[live(b, nu)], 0, chunk(b, c, nu))),
                  pl.BlockSpec((1, d, fc), lambda b, c, be, nu: (be[live(b, nu)], 0, chunk(b, c, nu))),
                  pl.BlockSpec((1, fc, d), lambda b, c, be, nu: (be[live(b, nu)], chunk(b, c, nu), 0))],
        out_specs=pl.BlockSpec((rb, d), lambda b, c, be, nu: (live(b, nu), 0)),
        scratch_shapes=[pltpu.VMEM((rb, d), F32)],
    )
    return pl.pallas_call(
        functools.partial(_moe_ffn_kernel, nch=nch, subs=subs),
        grid_spec=grid_spec,
        out_shape=jax.ShapeDtypeStruct((r, d), BF16),
        compiler_params=_cparams(("arbitrary", "arbitrary"), MOE_VMEM_LIMIT),
        name="moe_ffn",
    )(block_expert, n_used, xs, row_gate, wg, wu, wd)


def _gather_rows(x, idx):
    return x.at[idx].get(mode="promise_in_bounds")


def _moe(hn, eg, wg, wu, wd, *, rb, n_experts, moe_layer):
    n, d = hn.shape
    rows = TOP_K * n
    gates = eg[:, :TOP_K].reshape(-1)
    flat_e = eg[:, TOP_K:2 * TOP_K].astype(jnp.int32).reshape(-1)
    onehot = (flat_e[:, None] == jnp.arange(n_experts, dtype=jnp.int32)[None, :]).astype(jnp.int32)
    rank = jnp.cumsum(onehot, axis=0) - onehot
    count = jnp.sum(onehot, axis=0)
    first_sorted = jnp.cumsum(count) - count
    padded = ((count + rb - 1) // rb) * rb
    ends = jnp.cumsum(padded)
    offset = ends - padded
    pos = offset[flat_e] + jnp.take_along_axis(rank, flat_e[:, None], axis=1)[:, 0]
    nblk = pl.cdiv(rows, rb) + n_experts
    r_tot = nblk * rb
    block_start = jnp.arange(nblk, dtype=jnp.int32) * rb
    block_expert = jnp.minimum(jnp.sum((block_start[:, None] >= ends[None, :]).astype(jnp.int32), axis=1),
                               n_experts - 1).astype(jnp.int32)
    n_used = (ends[-1:] // rb).astype(jnp.int32)
    order = jnp.argsort(flat_e, stable=True).astype(jnp.int32)
    slot_e = jnp.repeat(block_expert, rb)
    local = jnp.arange(r_tot, dtype=jnp.int32) - offset[slot_e]
    valid = (local >= 0) & (local < count[slot_e])
    src_row = order[jnp.clip(first_sorted[slot_e] + local, 0, rows - 1)]
    src_tok = jnp.where(valid, src_row // TOP_K, 0)
    row_gate = jnp.where(valid, gates[src_row], 0.0)
    xs = _gather_rows(hn, src_tok)
    ys = _moe_ffn(xs, row_gate[:, None], block_expert + moe_layer * n_experts, n_used, wg, wu, wd, rb=rb)
    return _gather_rows(ys, pos.reshape(n, TOP_K).T.reshape(-1))


def _ple_kernel(*refs, with_moe, final):
    refs = list(refs)
    h_ref = refs.pop(0)
    y_refs = (refs.pop(0), refs.pop(0)) if with_moe else ()
    p_ref, g_ref, wg_ref, wp_ref = refs[:4]
    refs = refs[4:]
    gf_ref = refs.pop(0) if final else None
    o_ref = refs[0]
    h = h_ref[...]
    for y_ref in y_refs:
        h = h + y_ref[...].astype(F32)
    gate = jax.nn.sigmoid(jnp.dot(_rms(h, g_ref[...]).astype(BF16), wg_ref[...],
                                  preferred_element_type=F32))
    h = h + gate * jnp.dot(p_ref[0].astype(BF16), wp_ref[...], preferred_element_type=F32)
    if final:
        h = _rms(h, gf_ref[...])
    o_ref[...] = h


def _ple(h, y, p_all, g, wg, wp, g_final=None, *, layer):
    n, d = h.shape
    tm = TOKEN_TILE if n % TOKEN_TILE == 0 else n
    nt = n // tm
    tile = lambda width: pl.BlockSpec((tm, width), lambda i: (i, 0))
    in_specs, args = [tile(d)], [h]
    if y is not None:
        in_specs += [tile(d), pl.BlockSpec((tm, d), lambda i: (i + nt, 0))]
        args += [y, y]
    in_specs += [pl.BlockSpec((1, tm, p_all.shape[2]), lambda i: (layer, i, 0)),
                 _full((1, d)), _resident(wg.shape), _resident(wp.shape)]
    args += [p_all, g, wg, wp]
    if g_final is not None:
        in_specs.append(_full((1, d)))
        args.append(g_final)
    return pl.pallas_call(
        functools.partial(_ple_kernel, with_moe=y is not None, final=g_final is not None),
        grid=(nt,), in_specs=in_specs, out_specs=tile(d),
        out_shape=jax.ShapeDtypeStruct((n, d), F32),
        compiler_params=_cparams(("arbitrary",)),
        name="ple",
    )(*args)


def _prep_w_in(w_in, dims):
    faw, sw, rw, nh = dims["faw"], dims["sw"], dims["rw"], dims["heads"]
    sizes = (faw, faw, faw, nh, sw, sw, rw, rw, rw, rw)
    parts, o = [], 0
    for s in sizes:
        parts.append(w_in[..., o:o + s])
        o += s
    fa_q, fa_k, fa_v, fa_f, sgu_u, sgu_v, ret_q, ret_k, ret_v, ret_g = parts

    def rot_cols(w):
        lead = w.shape[:-1]
        wh = w.reshape(lead + (rw // HEAD_DIM, 2, HEAD_DIM // 2))
        return jnp.stack([-wh[..., 1, :], wh[..., 0, :]], axis=-2).reshape(lead + (rw,))

    f_pad = jnp.pad(fa_f, [(0, 0)] * (w_in.ndim - 1) + [(0, LANES - nh)])
    cols = [("q", fa_q), ("k", fa_k), ("v", fa_v), ("u", sgu_u), ("sv", sgu_v), ("rq", ret_q),
            ("rqr", rot_cols(ret_q)), ("rk", ret_k), ("rkr", rot_cols(ret_k)), ("rv", ret_v),
            ("rg", ret_g), ("f", f_pad)]
    offs, o = {}, 0
    for name, c in cols:
        offs[name] = (o, o + c.shape[-1])
        o += c.shape[-1]
    return jnp.concatenate([c for _, c in cols], axis=-1).astype(BF16), offs


def _rope_tables(pos, ret_heads):
    half = HEAD_DIM // 2
    inv = ROPE_BASE ** (-jnp.arange(half, dtype=F32) / half)
    ang = pos.astype(F32)[:, None] * inv[None, :]
    cos = jnp.tile(jnp.cos(ang), (1, 2 * ret_heads))
    sin = jnp.tile(jnp.sin(ang), (1, 2 * ret_heads))
    return cos, sin


def _pair_states(s):
    b, hds = s.shape[:2]
    s = s.reshape(b, hds // 2, 2, HEAD_DIM, HEAD_DIM)
    z = jnp.zeros_like(s[:, :, 0])
    top = jnp.concatenate([s[:, :, 0], z], axis=-1)
    bot = jnp.concatenate([z, s[:, :, 1]], axis=-1)
    return jnp.concatenate([top, bot], axis=-2)


def _unpair_states(sp):
    a = sp[:, :, :HEAD_DIM, :HEAD_DIM]
    b = sp[:, :, HEAD_DIM:, HEAD_DIM:]
    return jnp.stack([a, b], axis=2).reshape(sp.shape[0], -1, HEAD_DIM, HEAD_DIM)


def _trunk(x, p, cache, wts, dims):
    batch, t, d = x.shape
    n = batch * t
    depth = wts["w_in"].shape[0]
    heads, ret_heads = dims["heads"], dims["ret_heads"]
    has_past = cache is not None
    offset = cache[0].shape[2] if has_past else 0
    pos = offset + jnp.arange(t, dtype=jnp.int32)
    cos, sin = _rope_tables(pos, ret_heads)
    if t < TOKEN_TILE:
        cos, sin = jnp.tile(cos, (batch, 1)), jnp.tile(sin, (batch, 1))

    if has_past:
        ret_blk, ret_rows = LANES, LANES
        tabs = _retention_tables(ret_heads, t, ret_blk)
    else:
        ret_blk = RET_CHUNK if t % RET_CHUNK == 0 else t
        ret_rows = t
        tabs = _retention_tables(ret_heads, ret_blk, ret_blk)

    sgu_full = wts["sgu_w"].shape[2]
    sgu_len = sgu_full if t >= sgu_full else t

    h = x.reshape(n, d)
    p_all = p.reshape(depth, n, -1)
    new_lf, new_ret, new_sgu = [], [], []
    kbuf = jnp.zeros((depth, n * heads, HEAD_DIM), F32)
    vbuf = jnp.zeros((depth, n * heads, HEAD_DIM), F32)
    for i in range(depth):
        outs = _inproj(h, wts["g_mix"][i], wts["w_in"][i], wts["b_forget"][i], cos, sin,
                       wts["sgu_ln_g"][i], wts["sgu_ln_b"][i], kbuf, vbuf, layer=i,
                       offs=wts["offs"], dims=dims, seq_len=t, prompt=not has_past)
        if has_past:
            kbuf, vbuf, qb, kb, vb, lf, u, sv, rq, rk, rv, rg = outs
            oa = _fox_sample(qb, kb, vb, lf, cache[0][i].reshape(batch, offset, -1),
                             cache[1][i].reshape(batch, offset, -1), cache[2][i],
                             batch=batch, ts=t, heads=heads)
        else:
            kbuf, vbuf, qa, ka, vt, stats, lf, u, sv, rq, rk, rv, rg = outs
            oa = _fox_prompt(qa, ka, vt, stats, batch=batch, seq=t, heads=heads)

        sgu_w, sgu_b = wts["sgu_w"][i], wts["sgu_b"][i]
        groups = sgu_w.shape[0]
        w_l = sgu_w[:, :sgu_len, :sgu_len]
        b_l = jnp.repeat(sgu_b[:, :sgu_len].T, HEAD_DIM, axis=1)
        if sgu_len % 128 != 0:
            reps = n // sgu_len
            eye = jnp.eye(reps, dtype=F32)
            w_l = jnp.einsum("ab,gts->gatbs", eye, w_l).reshape(groups, n, n)
            b_l = jnp.tile(b_l, (reps, 1))
        ob = _sgu(u, sv, w_l, b_l)

        if has_past:
            padr = lambda a: jnp.pad(a.reshape(batch, t, -1), ((0, 0), (0, ret_rows - t), (0, 0))
                                     ).reshape(batch * ret_rows, -1)
            s0 = _pair_states(cache[3][i].astype(F32))
            oc, sfin = _retention(padr(rq), padr(rk), padr(rv), padr(rg), s0, tabs, wts["ret_gn_g"][i],
                                  batch=batch, rows_per_seq=ret_rows, blk=ret_blk)
            oc = oc.reshape(batch, ret_rows, -1)[:, :t].reshape(n, -1)
        else:
            s0 = jnp.zeros((batch, ret_heads // 2, LANES, LANES), F32)
            oc, sfin = _retention(rq, rk, rv, rg, s0, tabs, wts["ret_gn_g"][i],
                                  batch=batch, rows_per_seq=ret_rows, blk=ret_blk)

        j = i // 2
        is_moe = i % 2 == 1
        last = i == depth - 1
        g_final = wts["g_final"] if last else None
        if is_moe:
            router = (wts["w_router"][j], wts["b_router"][j], wts["n_experts"])
            h, hn, eg = _outproj(h, oa, ob, oc, wts["w_out"][i], wts["g_ffn"][i], router)
            rb = MOE_ROW_BLOCK if n >= 8 * MOE_ROW_BLOCK else LANES
            y = _moe(hn, eg, wts["w_exp_gate"], wts["w_exp_up"], wts["w_exp_down"], rb=rb,
                     n_experts=wts["n_experts"], moe_layer=j)
            h = _ple(h, y, p_all, wts["g_ple"][i], wts["w_ple_gate"][i],
                     wts["w_ple_proj"][i], g_final, layer=i)
        else:
            h, hn = _outproj(h, oa, ob, oc, wts["w_out"][i], wts["g_ffn"][i])
            h = _dense_ffn(h, hn, wts["w_dense_gate"][j], wts["w_dense_up"][j], wts["w_dense_down"][j])
            h = _ple(h, None, p_all, wts["g_ple"][i], wts["w_ple_gate"][i],
                     wts["w_ple_proj"][i], g_final, layer=i)

        new_lf.append(lf.reshape(batch, t, heads))
        new_ret.append(_unpair_states(sfin))
        new_sgu.append(sv.reshape(batch, t, -1))

    y = h.reshape(batch, t, d)
    kv_shape = (depth, batch, t, heads, HEAD_DIM)
    return (y, kbuf.reshape(kv_shape), vbuf.reshape(kv_shape), jnp.stack(new_lf), jnp.stack(new_ret),
            jnp.stack(new_sgu))


def kernel(x_prompt, x_sample, cache_fa_k, cache_fa_v, cache_fa_logf, state_ret, p_prompt, p_sample, g_mix, w_in, b_forget, sgu_ln_g, sgu_ln_b, sgu_w, sgu_b, ret_gn_g, w_out, g_ffn, w_dense_gate, w_dense_up, w_dense_down, w_router, b_router, w_exp_gate, w_exp_up, w_exp_down, g_ple, w_ple_gate, w_ple_proj, g_final):
    heads = cache_fa_k.shape[3]
    ret_heads = state_ret.shape[2]
    groups = sgu_w.shape[1]
    n_experts = w_router.shape[-1]
    dims = dict(heads=heads, ret_heads=ret_heads, faw=heads * HEAD_DIM, sw=groups * HEAD_DIM,
                rw=ret_heads * HEAD_DIM)
    assert cache_fa_k.shape[4] == HEAD_DIM and heads % 2 == 0 and ret_heads % 2 == 0

    w_in_b, offs = _prep_w_in(w_in, dims)
    row = lambda a: a[:, None, :].astype(F32)
    wr1 = w_router.astype(BF16)
    wr2 = (w_router - wr1.astype(F32)).astype(BF16)
    pad_e = lambda a: jnp.pad(a, [(0, 0)] * (a.ndim - 1) + [(0, LANES - n_experts)])
    wts = dict(
        offs=offs, n_experts=n_experts,
        g_mix=row(g_mix), w_in=w_in_b,
        b_forget=row(jnp.pad(b_forget, ((0, 0), (0, LANES - heads)))),
        sgu_ln_g=row(sgu_ln_g), sgu_ln_b=row(sgu_ln_b), sgu_w=sgu_w, sgu_b=sgu_b,
        ret_gn_g=row(ret_gn_g), w_out=w_out.astype(BF16), g_ffn=row(g_ffn),
        w_dense_gate=w_dense_gate.astype(BF16), w_dense_up=w_dense_up.astype(BF16),
        w_dense_down=w_dense_down.astype(BF16),
        w_router=jnp.concatenate([pad_e(wr1), pad_e(wr2)], axis=-1), b_router=row(pad_e(b_router)),
        w_exp_gate=w_exp_gate.astype(BF16).reshape((-1,) + w_exp_gate.shape[2:]),
        w_exp_up=w_exp_up.astype(BF16).reshape((-1,) + w_exp_up.shape[2:]),
        w_exp_down=w_exp_down.astype(BF16).reshape((-1,) + w_exp_down.shape[2:]),
        g_ple=row(g_ple), w_ple_gate=w_ple_gate.astype(BF16), w_ple_proj=w_ple_proj.astype(BF16),
        g_final=g_final[None, :].astype(F32),
    )

    y_p, k_p, v_p, lf_p, ret_p, _ = _trunk(x_prompt, p_prompt, None, wts, dims)
    y_s, k_s, v_s, lf_s, ret_s, sgu_s = _trunk(
        x_sample, p_sample, (cache_fa_k, cache_fa_v, cache_fa_logf, state_ret), wts, dims)
    return (y_p, y_s, k_p, v_p, lf_p, ret_p, k_s, v_s, lf_s, ret_s, sgu_s)
```

```python
import functools
import math

import jax
import jax.numpy as jnp
import numpy as np
from jax import lax
from jax.experimental import pallas as pl
from jax.experimental.pallas import tpu as pltpu

F32 = jnp.float32
BF16 = jnp.bfloat16

HEAD_DIM = 64
LANES = 128
MXU_DEPTH = 256
ROPE_BASE = 10000.0
RET_DECAY_MIN = 1.0 / 32.0
RET_DECAY_MAX = 1.0 / 512.0
NORM_EPS = 1e-6
GN_EPS = 1e-5
NEG_INF = -1e30
TOP_K = 2
LOG2E = 1.4426950408889634

TOKEN_TILE = 512
ATTN_TILE = 512
RET_CHUNK = 512
MOE_ROW_BLOCK = 1024
VMEM_LIMIT = 48 * 1024 * 1024
MOE_VMEM_LIMIT = 56 * 1024 * 1024


def _cparams(sem, vmem=VMEM_LIMIT):
    return pltpu.CompilerParams(dimension_semantics=sem, vmem_limit_bytes=vmem)


def _full(shape):
    zeros = (0,) * len(shape)
    return pl.BlockSpec(shape, lambda *_: zeros)


def _resident(shape):
    zeros = (0,) * len(shape)
    return pl.BlockSpec(shape, lambda *_: zeros, pipeline_mode=pl.Buffered(1))


def _rms(x, g):
    return x * lax.rsqrt(jnp.mean(x * x, axis=-1, keepdims=True) + NORM_EPS) * g


def _silu(x):
    return x * (1.0 / (1.0 + jnp.exp(-x)))


def _split3(x):
    p1 = x.astype(BF16)
    r1 = x - p1.astype(F32)
    p2 = r1.astype(BF16)
    p3 = (r1 - p2.astype(F32)).astype(BF16)
    return p1, p2, p3


def _nt_dot(a, b):
    return lax.dot_general(a, b, (((1,), (1,)), ((), ())), preferred_element_type=F32)


def _inproj_kernel(*refs, offs, tiles_per_seq, prompt, tm, heads):
    (h_ref, g_ref, w_ref, bf_ref, cos_ref, sin_ref, lng_ref, lnb_ref) = refs[:8]
    if prompt:
        (selk_ref, onek_ref, selq_ref, oneq_ref, hsel_ref) = refs[10:15]
        (k32_ref, v32_ref, qa_ref, ka_ref, vt_ref, st_ref, lf_ref, u_ref, sv_ref,
         rq_ref, rk_ref, rv_ref, rg_ref, carry_ref) = refs[15:]
    else:
        (k32_ref, v32_ref, q_ref, kb_ref, vb_ref, lf_ref, u_ref, sv_ref,
         rq_ref, rk_ref, rv_ref, rg_ref) = refs[10:]

    xn = _rms(h_ref[...], g_ref[...]).astype(BF16)

    def proj(name):
        a, b = offs[name]
        return jnp.dot(xn, w_ref[:, a:b], preferred_element_type=F32)

    q = proj("q") * (HEAD_DIM ** -0.5)
    k = proj("k")
    v = proj("v")
    for hd in range(heads):
        hs = slice(hd * HEAD_DIM, (hd + 1) * HEAD_DIM)
        k32_ref[0, pl.ds(hd, tm, stride=heads), :] = k[:, hs]
        v32_ref[0, pl.ds(hd, tm, stride=heads), :] = v[:, hs]
    if not prompt:
        q_ref[...] = q.astype(BF16)
        kb_ref[...] = k.astype(BF16)
        vb_ref[...] = v.astype(BF16)

    u_ref[...] = jax.nn.gelu(proj("u"))
    sv = jax.nn.gelu(proj("sv"))
    mu = jnp.mean(sv, axis=-1, keepdims=True)
    var = jnp.mean(jnp.square(sv - mu), axis=-1, keepdims=True)
    sv_ref[...] = (sv - mu) * lax.rsqrt(var + GN_EPS) * lng_ref[...] + lnb_ref[...]

    cos = cos_ref[...]
    sin = sin_ref[...]
    rq_ref[...] = (proj("rq") * cos + proj("rqr") * sin).astype(BF16)
    rk_ref[...] = ((proj("rk") * cos + proj("rkr") * sin) * (HEAD_DIM ** -0.5)).astype(BF16)
    rv_ref[...] = proj("rv").astype(BF16)
    rg_ref[...] = _silu(proj("rg")).astype(BF16)

    f = proj("f") + bf_ref[...]
    lf = jnp.minimum(f, 0.0) - jnp.log1p(jnp.exp(-jnp.abs(f)))
    nh = lf_ref.shape[-1]
    lf_ref[...] = lf[:, :nh]

    if prompt:
        i = pl.program_id(0)

        @pl.when(i % tiles_per_seq == 0)
        def _():
            carry_ref[...] = jnp.zeros_like(carry_ref)

        row = lax.broadcasted_iota(jnp.int32, (tm, tm), 0)
        col = lax.broadcasted_iota(jnp.int32, (tm, tm), 1)
        tril = jnp.where(col <= row, 1.0, 0.0).astype(BF16)
        pieces = jnp.concatenate(_split3(lf), axis=1)
        cs = jnp.dot(tril, pieces, preferred_element_type=F32)
        fc = cs[:, :LANES] + cs[:, LANES:2 * LANES] + cs[:, 2 * LANES:] + carry_ref[...]
        carry_ref[...] = fc[tm - 1:tm, :]

        fpieces = jnp.concatenate(_split3(fc * LOG2E), axis=1)
        kaug = jnp.dot(fpieces, selk_ref[...], preferred_element_type=F32) + onek_ref[...]
        qaug = jnp.dot(fpieces, selq_ref[...], preferred_element_type=F32) + oneq_ref[...]

        sq = jnp.concatenate([q * q, k * k], axis=1) * (1.0 + 2.0 ** -7)
        n2 = jnp.max(jnp.dot(sq.astype(BF16), hsel_ref[...], preferred_element_type=F32),
                     axis=0, keepdims=True)
        st_ref[0] = jnp.concatenate([n2[:, :LANES], n2[:, LANES:], fc[0:1, :], fc[tm - 1:tm, :],
                                     jnp.zeros((4, LANES), F32)], axis=0)

        lane_head = lax.broadcasted_iota(jnp.int32, (1, LANES), 1) // HEAD_DIM
        q2 = q * LOG2E
        for p in range(heads // 2):
            sl = slice(p * LANES, (p + 1) * LANES)
            ka_ref[p, :, :LANES] = k[:, sl].astype(BF16)
            ka_ref[p, :, LANES:] = kaug[:, sl].astype(BF16)
            for hh in range(2):
                hd = 2 * p + hh
                own = lane_head == hh
                qa_ref[hd, :, :LANES] = jnp.where(own, q2[:, sl], 0.0).astype(BF16)
                qa_ref[hd, :, LANES:] = qaug[:, hd * LANES:(hd + 1) * LANES].astype(BF16)
                vt_ref[hd, 0] = jnp.where(own, v[:, sl], 1.0).T.astype(BF16)


def _bias_selectors(heads):
    npairs = heads // 2
    selk = np.zeros((3 * LANES, npairs * LANES), np.float32)
    onek = np.zeros((1, npairs * LANES), np.float32)
    selq = np.zeros((3 * LANES, heads * LANES), np.float32)
    oneq = np.zeros((1, heads * LANES), np.float32)
    for p in range(npairs):
        for a in range(3):
            for hh in range(2):
                selk[a * LANES + 2 * p + hh, p * LANES + 3 * hh + a] = -1.0
            onek[0, p * LANES + 6 + a] = 1.0
    for hd in range(heads):
        for a in range(3):
            selq[a * LANES + hd, hd * LANES + 6 + a] = 1.0
            oneq[0, hd * LANES + 3 * (hd % 2) + a] = 1.0
    faw = heads * HEAD_DIM
    col = np.concatenate([np.arange(faw) // HEAD_DIM, LANES + np.arange(faw) // HEAD_DIM])
    hsel = col[:, None] == np.arange(2 * LANES)[None, :]
    return (jnp.asarray(selk, BF16), jnp.asarray(onek), jnp.asarray(selq, BF16), jnp.asarray(oneq),
            jnp.asarray(hsel, BF16))


def _inproj(h, g, w, bf, cos, sin, lng, lnb, kbuf, vbuf, *, layer, offs, dims, seq_len, prompt):
    n, d = h.shape
    tm = TOKEN_TILE if n % TOKEN_TILE == 0 else n
    nt = n // tm
    tiles_per_seq = max(seq_len // tm, 1)
    tab_blocks = cos.shape[0] // tm
    faw, sw, rw, nh = dims["faw"], dims["sw"], dims["rw"], dims["heads"]
    npairs = nh // 2
    tile = lambda width: pl.BlockSpec((tm, width), lambda i: (i, 0))
    tab = pl.BlockSpec((tm, rw), lambda i: (i % tab_blocks, 0))
    hbm = pl.BlockSpec(memory_space=pl.ANY)
    in_specs = [tile(d), _full((1, d)), _resident(w.shape), _full((1, LANES)), tab, tab,
                _full((1, sw)), _full((1, sw)), hbm, hbm]
    args = [h, g, w, bf, cos, sin, lng, lnb, kbuf, vbuf]
    layer_rows = pl.BlockSpec((1, tm * nh, HEAD_DIM), lambda i: (layer, i, 0))
    out_shape = [jax.ShapeDtypeStruct(kbuf.shape, F32), jax.ShapeDtypeStruct(vbuf.shape, F32)]
    out_specs = [layer_rows, layer_rows]
    bf_rows = jax.ShapeDtypeStruct((n, faw), BF16)
    if prompt:
        assert tm == ATTN_TILE
        sels = _bias_selectors(nh)
        in_specs += [_full(s.shape) for s in sels]
        args += list(sels)
        out_shape += [jax.ShapeDtypeStruct((nh, n, 2 * LANES), BF16),
                      jax.ShapeDtypeStruct((npairs, n, 2 * LANES), BF16),
                      jax.ShapeDtypeStruct((nh, nt, LANES, tm), BF16),
                      jax.ShapeDtypeStruct((nt, 8, LANES), F32)]
        out_specs += [pl.BlockSpec((nh, tm, 2 * LANES), lambda i: (0, i, 0)),
                      pl.BlockSpec((npairs, tm, 2 * LANES), lambda i: (0, i, 0)),
                      pl.BlockSpec((nh, 1, LANES, tm), lambda i: (0, i, 0, 0)),
                      pl.BlockSpec((1, 8, LANES), lambda i: (i, 0, 0))]
    else:
        out_shape += [bf_rows, bf_rows, bf_rows]
        out_specs += [tile(faw)] * 3
    out_shape += [jax.ShapeDtypeStruct((n, nh), F32),
                  jax.ShapeDtypeStruct((n, sw), F32), jax.ShapeDtypeStruct((n, sw), F32)]
    out_specs += [tile(nh), tile(sw), tile(sw)]
    out_shape += [jax.ShapeDtypeStruct((n, rw), BF16)] * 4
    out_specs += [tile(rw)] * 4
    scratch = [pltpu.VMEM((1, LANES), F32)] if prompt else []
    return pl.pallas_call(
        functools.partial(_inproj_kernel, offs=offs, tiles_per_seq=tiles_per_seq,
                          prompt=prompt, tm=tm, heads=nh),
        grid=(nt,), in_specs=in_specs, out_specs=out_specs, out_shape=out_shape,
        scratch_shapes=scratch, input_output_aliases={8: 0, 9: 1},
        compiler_params=_cparams(("arbitrary",)),
        name="inproj",
    )(*args)


def _fox_prompt_kernel(first_ref, fast_ref, q_ref, k_ref, vt_ref, o_ref, m_ref, acc_ref, *, t):
    qi = pl.program_id(2)
    idx = (pl.program_id(0) * pl.num_programs(1) + pl.program_id(1)) * pl.num_programs(2) + qi
    first = first_ref[idx]
    acc_ref[...] = jnp.zeros(acc_ref.shape, F32)

    def scores(ki, hh, masked):
        start = pl.multiple_of(ki * t, t)
        s = _nt_dot(k_ref[0, pl.ds(start, t), :], q_ref[hh])
        if masked:
            key = lax.broadcasted_iota(jnp.int32, (t, t), 0)
            qry = lax.broadcasted_iota(jnp.int32, (t, t), 1)
            s = jnp.where(key <= qry, s, NEG_INF)
        return s

    def plain_step(ki, masked):
        pr = [jnp.exp2(scores(ki, hh, masked)).astype(BF16) for hh in range(2)]
        for hh in range(2):
            acc_ref[hh] += jnp.dot(vt_ref[hh, ki], pr[hh], preferred_element_type=F32)

    def online_step(ki, masked):
        for hh in range(2):
            s = scores(ki, hh, masked)
            m_prev = m_ref[hh]
            m_new = jnp.maximum(m_prev, jnp.max(s, axis=0, keepdims=True))
            pr = jnp.exp2(s - m_new).astype(BF16)
            acc_ref[hh] = (jnp.exp2(m_prev - m_new) * acc_ref[hh]
                           + jnp.dot(vt_ref[hh, ki], pr, preferred_element_type=F32))
            m_ref[hh] = m_new

    def run(step):
        def body(ki, carry):
            step(ki, False)
            return carry

        lax.fori_loop(first, qi, body, 0)
        step(qi, True)

    bounded = fast_ref[idx] == 1

    @pl.when(bounded)
    def _():
        run(plain_step)

    @pl.when(jnp.logical_not(bounded))
    def _():
        m_ref[...] = jnp.full(m_ref.shape, NEG_INF, F32)
        run(online_step)

    acc_a = acc_ref[0]
    acc_b = acc_ref[1]
    head_a = lax.broadcasted_iota(jnp.int32, (LANES, 1), 0) < HEAD_DIM
    o_t = jnp.where(head_a, acc_a / acc_a[HEAD_DIM:HEAD_DIM + 1, :], acc_b / acc_b[0:1, :])
    o_ref[...] = o_t.T.astype(BF16)


F32_EXP_ZERO = -104.0
PLAIN_SCORE_LIMIT = 30.0


def _attention_plan(stats, *, batch, nq, heads):
    st = stats.reshape(batch, nq, 8, LANES)[..., :heads]
    grow = 1.0 + 2.0 ** -6
    qn = jnp.sqrt(st[:, :, 0]) * grow
    kn = jnp.sqrt(jnp.max(st[:, :, 1], axis=1, keepdims=True)) * grow
    u = qn * kn
    plain = jnp.all((u <= PLAIN_SCORE_LIMIT).reshape(batch, nq, heads // 2, 2), axis=-1)
    plain_h = jnp.repeat(plain, 2, axis=-1)
    reach = jnp.where(plain_h, u, 2.0 * u)
    f_first, f_last = st[:, :, 2], st[:, :, 3]
    bound = reach[:, :, None, :] + f_first[:, :, None, :] - f_last[:, None, :, :] + 1.0
    past = (jnp.arange(nq)[None, :] < jnp.arange(nq)[:, None])[None, :, :, None]
    dead = jnp.argmax(~(past & (bound < F32_EXP_ZERO)), axis=2).astype(jnp.int32)
    dead = jnp.min(dead.reshape(batch, nq, heads // 2, 2), axis=-1)
    flat = lambda a: a.transpose(0, 2, 1).reshape(-1).astype(jnp.int32)
    return flat(dead), flat(plain)


def _fox_prompt(qa, ka, vt, stats, *, batch, seq, heads):
    n = qa.shape[1]
    t = ATTN_TILE
    nq = seq // t
    npairs = heads // 2
    first, plain = _attention_plan(stats, batch=batch, nq=nq, heads=heads)
    grid_spec = pltpu.PrefetchScalarGridSpec(
        num_scalar_prefetch=2,
        grid=(batch, npairs, nq),
        in_specs=[pl.BlockSpec((2, t, 2 * LANES), lambda b, p, i, *_: (p, b * nq + i, 0)),
                  pl.BlockSpec((1, seq, 2 * LANES), lambda b, p, i, *_: (p, b, 0)),
                  pl.BlockSpec((2, nq, LANES, t), lambda b, p, i, *_: (p, b, 0, 0))],
        out_specs=pl.BlockSpec((t, LANES), lambda b, p, i, *_: (b * nq + i, p)),
        scratch_shapes=[pltpu.VMEM((2, 1, t), F32), pltpu.VMEM((2, LANES, t), F32)],
    )
    return pl.pallas_call(
        functools.partial(_fox_prompt_kernel, t=t),
        grid_spec=grid_spec,
        out_shape=jax.ShapeDtypeStruct((n, heads * HEAD_DIM), BF16),
        compiler_params=_cparams(("arbitrary", "arbitrary", "arbitrary")),
        name="fox_prompt",
    )(first, plain, qa, ka, vt)


def _fox_sample_kernel(q_ref, kn_ref, vn_ref, pk_ref, pv_ref, plft_ref, lf_ref, lft_ref, o_ref,
                       *, heads, ts, past):
    hi = lax.Precision.HIGHEST
    r_ = lax.broadcasted_iota(jnp.int32, (past, past), 0)
    c_ = lax.broadcasted_iota(jnp.int32, (past, past), 1)
    after = jnp.where(r_ > c_, 1.0, 0.0).astype(F32)
    g_row = jnp.dot(plft_ref[0], after, precision=hi, preferred_element_type=F32)
    rr = lax.broadcasted_iota(jnp.int32, (LANES, LANES), 0)
    cc = lax.broadcasted_iota(jnp.int32, (LANES, LANES), 1)
    tril = jnp.where(cc <= rr, 1.0, 0.0).astype(F32)
    triu = jnp.where(rr <= cc, 1.0, 0.0).astype(F32)
    c_col = jnp.dot(tril, lf_ref[0], precision=hi, preferred_element_type=F32)[:ts]
    c_row = jnp.dot(lft_ref[0], triu, precision=hi, preferred_element_type=F32)
    visible = (lax.broadcasted_iota(jnp.int32, (ts, LANES), 1)
               <= lax.broadcasted_iota(jnp.int32, (ts, LANES), 0))
    lane = lax.broadcasted_iota(jnp.int32, (1, LANES), 1)
    lo = lane < HEAD_DIM
    for p in range(heads // 2):
        sl = slice(p * LANES, (p + 1) * LANES)
        q2 = q_ref[:, sl]
        kp2 = pk_ref[0, :, sl].astype(BF16)
        vp2 = pv_ref[0, :, sl].astype(BF16)
        kn2 = kn_ref[:, sl]
        vn2 = vn_ref[:, sl]
        zero = jnp.zeros_like(q2)
        res = []
        for hh in range(2):
            h = 2 * p + hh
            qh = jnp.where(lo, q2, zero) if hh == 0 else jnp.where(lo, zero, q2)
            s_p = _nt_dot(qh, kp2) + c_col[:, h:h + 1] + g_row[h:h + 1, :]
            s_n = _nt_dot(qh, kn2) + c_col[:, h:h + 1] - c_row[h:h + 1, :]
            s_n = jnp.where(visible, s_n, NEG_INF)
            m = jnp.maximum(jnp.max(s_p, axis=1, keepdims=True), jnp.max(s_n, axis=1, keepdims=True))
            pp = jnp.exp(s_p - m)
            pn = jnp.exp(s_n - m)
            l = jnp.sum(pp, axis=1, keepdims=True) + jnp.sum(pn, axis=1, keepdims=True)
            o = (jnp.dot(pp.astype(BF16), vp2, preferred_element_type=F32)
                 + jnp.dot(pn.astype(BF16), vn2, preferred_element_type=F32))
            res.append(o / l)
        o_ref[:, sl] = jnp.where(lo, res[0], res[1]).astype(BF16)


def _fox_sample(qb, kb, vb, lf, past_k, past_v, past_lf, *, batch, ts, heads):
    n, faw = qb.shape
    past = past_k.shape[1]
    plft = past_lf.transpose(0, 2, 1)
    lf3 = lf.reshape(batch, ts, heads)
    lfp = jnp.pad(lf3, ((0, 0), (0, LANES - ts), (0, LANES - heads)))
    lftp = jnp.pad(lf3.transpose(0, 2, 1), ((0, 0), (0, 0), (0, LANES - ts)))
    padr = lambda a: jnp.pad(a.reshape(batch, ts, faw), ((0, 0), (0, LANES - ts), (0, 0))
                             ).reshape(batch * LANES, faw)
    row = lambda rows: pl.BlockSpec((rows, faw), lambda b: (b, 0))
    return pl.pallas_call(
        functools.partial(_fox_sample_kernel, heads=heads, ts=ts, past=past),
        grid=(batch,),
        in_specs=[row(ts), row(LANES), row(LANES),
                  pl.BlockSpec((1, past, faw), lambda b: (b, 0, 0)),
                  pl.BlockSpec((1, past, faw), lambda b: (b, 0, 0)),
                  pl.BlockSpec((1, heads, past), lambda b: (b, 0, 0)),
                  pl.BlockSpec((1, LANES, LANES), lambda b: (b, 0, 0)),
                  pl.BlockSpec((1, heads, LANES), lambda b: (b, 0, 0))],
        out_specs=row(ts),
        out_shape=jax.ShapeDtypeStruct((n, faw), BF16),
        compiler_params=_cparams(("arbitrary",)),
        name="fox_sample",
    )(qb, padr(kb), padr(vb), past_k, past_v, plft, lfp, lftp)


def _sgu_kernel(u_ref, v_ref, w_ref, b_ref, o_ref, *, chunk, chunks_per_tile, groups):
    width = u_ref.shape[-1]
    row = lax.broadcasted_iota(jnp.int32, (chunk, chunk), 0)
    col = lax.broadcasted_iota(jnp.int32, (chunk, chunk), 1)
    causal = col <= row
    grp = lax.broadcasted_iota(jnp.int32, (1, width), 1) // HEAD_DIM
    wm = [jnp.where(causal, w_ref[g], 0.0).astype(BF16) for g in range(groups)]
    bias = b_ref[...]
    for c in range(chunks_per_tile):
        sl = slice(c * chunk, (c + 1) * chunk)
        vc = v_ref[sl, :].astype(BF16)
        mixed = jnp.zeros((chunk, width), F32)
        for g in range(groups):
            mixed = jnp.where(grp == g, jnp.dot(wm[g], vc, preferred_element_type=F32), mixed)
        o_ref[sl, :] = (u_ref[sl, :] * (mixed + bias)).astype(BF16)


def _sgu(u, v, w, bias):
    n, width = u.shape
    groups, chunk, _ = w.shape
    tile = TOKEN_TILE if (n % TOKEN_TILE == 0 and TOKEN_TILE % chunk == 0) else chunk
    tok = pl.BlockSpec((tile, width), lambda i: (i, 0))
    return pl.pallas_call(
        functools.partial(_sgu_kernel, chunk=chunk, chunks_per_tile=tile // chunk, groups=groups),
        grid=(n // tile,),
        in_specs=[tok, tok, _full(w.shape), _full(bias.shape)],
        out_specs=tok,
        out_shape=jax.ShapeDtypeStruct((n, width), BF16),
        compiler_params=_cparams(("arbitrary",)),
        name="sgu",
    )(u, v, w, bias)


def _retention_kernel(q_ref, k_ref, v_ref, g_ref, s0_ref, dmat_ref, qdec_ref, kdec_ref, sdec_ref,
                      bmask_ref, avg_ref, gn_ref, o_ref, sfin_ref, s_ref, *, npairs):
    c = pl.program_id(1)

    @pl.when(c == 0)
    def _():
        s_ref[...] = s0_ref[0]

    lane = lax.broadcasted_iota(jnp.int32, (1, LANES), 1)
    lo = lane < HEAD_DIM
    avg = avg_ref[...]

    def group_mean(x):
        x1 = x.astype(BF16)
        x2 = (x - x1.astype(F32)).astype(BF16)
        return (jnp.dot(x1, avg, preferred_element_type=F32)
                + jnp.dot(x2, avg, preferred_element_type=F32))

    for p in range(npairs):
        sl = slice(p * LANES, (p + 1) * LANES)
        q2 = q_ref[:, sl]
        k2 = k_ref[:, sl]
        v2 = v_ref[:, sl]
        zero = jnp.zeros_like(q2)
        inner = []
        for hh in range(2):
            qh = jnp.where(lo, q2, zero) if hh == 0 else jnp.where(lo, zero, q2)
            sc = _nt_dot(qh, k2) * dmat_ref[2 * p + hh]
            inner.append(jnp.dot(sc.astype(BF16), v2, preferred_element_type=F32))
        state = s_ref[p]
        cross = jnp.dot(q2, state.astype(BF16), preferred_element_type=F32) * qdec_ref[:, sl]
        o = jnp.where(lo, inner[0], inner[1]) + cross
        kd_t = (k2.astype(F32) * kdec_ref[:, sl]).T.astype(BF16)
        s_ref[p] = state * sdec_ref[p] + jnp.dot(kd_t, v2, preferred_element_type=F32) * bmask_ref[...]
        mu = group_mean(o)
        d = o - mu
        var = group_mean(d * d)
        on = d * lax.rsqrt(var + GN_EPS)
        o_ref[:, sl] = (on * gn_ref[:, sl] * g_ref[:, sl].astype(F32)).astype(BF16)

    sfin_ref[0] = s_ref[...]


def _retention(rq, rk, rv, rg, s0, tabs, gn, *, batch, rows_per_seq, blk):
    n, rw = rq.shape
    npairs = rw // LANES
    nc = rows_per_seq // blk
    tok = pl.BlockSpec((blk, rw), lambda b, c: (b * nc + c, 0))
    st = pl.BlockSpec((1, npairs, LANES, LANES), lambda b, c: (b, 0, 0, 0))
    dmat, qdec, kdec, sdec, bmask, avg = tabs
    return pl.pallas_call(
        functools.partial(_retention_kernel, npairs=npairs),
        grid=(batch, nc),
        in_specs=[tok, tok, tok, tok, st, _full(dmat.shape), _full(qdec.shape), _full(kdec.shape),
                  _full(sdec.shape), _full(bmask.shape), _full(avg.shape), _full(gn.shape)],
        out_specs=[tok, st],
        out_shape=[jax.ShapeDtypeStruct((n, rw), BF16),
                   jax.ShapeDtypeStruct((batch, npairs, LANES, LANES), F32)],
        scratch_shapes=[pltpu.VMEM((npairs, LANES, LANES), F32)],
        compiler_params=_cparams(("arbitrary", "arbitrary")),
        name="retention",
    )(rq, rk, rv, rg, s0, dmat, qdec, kdec, sdec, bmask, avg, gn)


def _retention_tables(ret_heads, true_len, blk):
    log_g = jnp.log1p(-jnp.exp(jnp.linspace(math.log(RET_DECAY_MIN), math.log(RET_DECAY_MAX),
                                            ret_heads, dtype=F32)))
    n = jnp.arange(blk, dtype=F32)
    diff = n[:, None] - n[None, :]
    causal = diff >= 0
    dmat = jnp.where(causal[None], jnp.exp(jnp.where(causal, diff, 0.0)[None] * log_g[:, None, None]), 0.0)
    per_lane = jnp.repeat(log_g, HEAD_DIM)
    qdec = jnp.exp((n[:, None] + 1.0) * per_lane[None, :])
    kdec = jnp.exp((true_len - 1.0 - n)[:, None] * per_lane[None, :])
    npairs = ret_heads // 2
    lane_head = jnp.arange(LANES) // HEAD_DIM
    bmask = (lane_head[:, None] == lane_head[None, :]).astype(F32)
    sdec = jnp.exp(true_len * per_lane).reshape(npairs, LANES)[:, :, None] * bmask[None]
    avg = (bmask / HEAD_DIM).astype(BF16)
    return dmat, qdec, kdec, sdec, bmask, avg


def _outproj_kernel(*refs, with_router, n_experts, faw, sw):
    if with_router:
        (h_ref, oa_ref, ob_ref, oc_ref, w_ref, g_ref, wr_ref, br_ref, hnew_ref, hn_ref, eg_ref) = refs
    else:
        (h_ref, oa_ref, ob_ref, oc_ref, w_ref, g_ref, hnew_ref, hn_ref) = refs
    o = (jnp.dot(oa_ref[...], w_ref[:faw, :], preferred_element_type=F32)
         + jnp.dot(ob_ref[...], w_ref[faw:faw + sw, :], preferred_element_type=F32)
         + jnp.dot(oc_ref[...], w_ref[faw + sw:, :], preferred_element_type=F32))
    h = h_ref[...] + o
    hnew_ref[...] = h
    hn = _rms(h, g_ref[...])
    hn_ref[...] = hn.astype(BF16)
    if with_router:
        x1 = hn.astype(BF16)
        x2 = (hn - x1.astype(F32)).astype(BF16)
        a = jnp.dot(x1, wr_ref[...], preferred_element_type=F32)
        b = jnp.dot(x2, wr_ref[:, :LANES], preferred_element_type=F32)
        logits = a[:, :LANES] + a[:, LANES:] + b + br_ref[...]
        lane = lax.broadcasted_iota(jnp.int32, logits.shape, 1)
        lanef = lane.astype(F32)
        logits = jnp.where(lane < n_experts, logits, -jnp.inf)
        m1 = jnp.max(logits, axis=1, keepdims=True)
        i1 = jnp.min(jnp.where(logits == m1, lanef, float(LANES)), axis=1, keepdims=True)
        rest = jnp.where(lanef == i1, -jnp.inf, logits)
        m2 = jnp.max(rest, axis=1, keepdims=True)
        i2 = jnp.min(jnp.where(rest == m2, lanef, float(LANES)), axis=1, keepdims=True)
        e2 = jnp.exp(m2 - m1)
        g1 = 1.0 / (1.0 + e2)
        g2 = e2 / (1.0 + e2)
        eg_ref[...] = jnp.where(lane == 0, g1, jnp.where(lane == 1, g2,
                                jnp.where(lane == 2, i1, jnp.where(lane == 3, i2, 0.0))))


def _outproj(h, oa, ob, oc, w, g, router=None):
    n, d = h.shape
    tm = TOKEN_TILE if n % TOKEN_TILE == 0 else n
    faw, sw, rw = oa.shape[1], ob.shape[1], oc.shape[1]
    tile = lambda width: pl.BlockSpec((tm, width), lambda i: (i, 0))
    in_specs = [tile(d), tile(faw), tile(sw), tile(rw), _resident(w.shape), _full((1, d))]
    args = [h, oa, ob, oc, w, g]
    out_shape = [jax.ShapeDtypeStruct((n, d), F32), jax.ShapeDtypeStruct((n, d), BF16)]
    out_specs = [tile(d), tile(d)]
    n_experts = 0
    if router is not None:
        wr, br, n_experts = router
        in_specs += [_full(wr.shape), _full(br.shape)]
        args += [wr, br]
        out_shape.append(jax.ShapeDtypeStruct((n, LANES), F32))
        out_specs.append(tile(LANES))
    return pl.pallas_call(
        functools.partial(_outproj_kernel, with_router=router is not None, n_experts=n_experts,
                          faw=faw, sw=sw),
        grid=(n // tm,), in_specs=in_specs, out_specs=out_specs, out_shape=out_shape,
        compiler_params=_cparams(("arbitrary",)),
        name="outproj",
    )(*args)


def _dense_tail_kernel(*refs, final, fc, nchunks, faw, sw):
    (h_ref, oa_ref, ob_ref, oc_ref, wo_ref, gffn_ref, wg_ref, wu_ref, wd_ref,
     p_ref, gple_ref, wpg_ref, wpp_ref) = refs[:13]
    rest = list(refs[13:])
    gf_ref = rest.pop(0) if final else None
    o_ref, acc_ref = rest
    o = (jnp.dot(oa_ref[...], wo_ref[:faw, :], preferred_element_type=F32)
         + jnp.dot(ob_ref[...], wo_ref[faw:faw + sw, :], preferred_element_type=F32)
         + jnp.dot(oc_ref[...], wo_ref[faw + sw:, :], preferred_element_type=F32))
    h = h_ref[...] + o
    x = _rms(h, gffn_ref[...]).astype(BF16)
    for c in range(nchunks):
        sl = slice(c * fc, (c + 1) * fc)
        g = jnp.dot(x, wg_ref[:, sl], preferred_element_type=F32)
        u = jnp.dot(x, wu_ref[:, sl], preferred_element_type=F32)
        a = (_silu(g) * u).astype(BF16)
        y = jnp.dot(a, wd_ref[sl, :], preferred_element_type=F32)
        if c == 0:
            acc_ref[...] = y
        else:
            acc_ref[...] += y
    h = h + acc_ref[...]
    gate = jax.nn.sigmoid(jnp.dot(_rms(h, gple_ref[...]).astype(BF16), wpg_ref[...],
                                  preferred_element_type=F32))
    h = h + gate * jnp.dot(p_ref[0].astype(BF16), wpp_ref[...], preferred_element_type=F32)
    if final:
        h = _rms(h, gf_ref[...])
    o_ref[...] = h


def _dense_tail(h, oa, ob, oc, wo, gffn, wg, wu, wd, p_all, gple, wpg, wpp, g_final=None, *, layer):
    n, d = h.shape
    ff = wg.shape[1]
    tm = TOKEN_TILE if n % TOKEN_TILE == 0 else n
    fc = MXU_DEPTH if ff % MXU_DEPTH == 0 else ff
    faw, sw, rw = oa.shape[1], ob.shape[1], oc.shape[1]
    tile = lambda width: pl.BlockSpec((tm, width), lambda i: (i, 0))
    in_specs = [tile(d), tile(faw), tile(sw), tile(rw), _resident(wo.shape), _full((1, d)),
                _resident(wg.shape), _resident(wu.shape), _resident(wd.shape),
                pl.BlockSpec((1, tm, p_all.shape[2]), lambda i: (layer, i, 0)), _full((1, d)),
                _resident(wpg.shape), _resident(wpp.shape)]
    args = [h, oa, ob, oc, wo, gffn, wg, wu, wd, p_all, gple, wpg, wpp]
    if g_final is not None:
        in_specs.append(_full((1, d)))
        args.append(g_final)
    return pl.pallas_call(
        functools.partial(_dense_tail_kernel, final=g_final is not None, fc=fc, nchunks=ff // fc,
                          faw=faw, sw=sw),
        grid=(n // tm,), in_specs=in_specs, out_specs=tile(d),
        out_shape=jax.ShapeDtypeStruct((n, d), F32),
        scratch_shapes=[pltpu.VMEM((tm, d), F32)],
        compiler_params=_cparams(("arbitrary",)),
        name="dense_tail",
    )(*args)


def _moe_ffn_kernel(be_ref, nu_ref, x_ref, wg_ref, wu_ref, wd_ref, o_ref, acc_ref, *, nch, subs):
    b = pl.program_id(0)
    c = pl.program_id(1)

    @pl.when(b < nu_ref[0])
    def _():
        @pl.when(c == 0)
        def _():
            acc_ref[...] = jnp.zeros_like(acc_ref)

        x = x_ref[...]
        for a0, a1 in subs:
            g = jnp.dot(x, wg_ref[0, :, a0:a1], preferred_element_type=F32)
            u = jnp.dot(x, wu_ref[0, :, a0:a1], preferred_element_type=F32)
            a = (_silu(g) * u).astype(BF16)
            acc_ref[...] += jnp.dot(a, wd_ref[0, a0:a1, :], preferred_element_type=F32)

        @pl.when(c == nch - 1)
        def _():
            o_ref[...] = acc_ref[...].astype(BF16)


def _moe_ffn(xs, block_expert, n_used, wg, wu, wd, *, rb):
    r, d = xs.shape
    ff = wg.shape[2]
    nblk = r // rb
    nch = 2 if ff % (2 * MXU_DEPTH) == 0 else 1
    fc = ff // nch
    step = 2 * MXU_DEPTH
    subs = tuple((a0, min(a0 + step, fc)) for a0 in range(0, fc, step))

    def live(b, nu):
        return jnp.minimum(b, nu[0] - 1)

    def chunk(b, c, nu):
        return jnp.where(b < nu[0], c, nch - 1)

    grid_spec = pltpu.PrefetchScalarGridSpec(
        num_scalar_prefetch=2,
        grid=(nblk, nch),
        in_specs=[pl.BlockSpec((rb, d), lambda b, c, be, nu: (live(b, nu), 0)),
                  pl.BlockSpec((1, d, fc), lambda b, c, be, nu: (be[live(b, nu)], 0, chunk(b, c, nu))),
                  pl.BlockSpec((1, d, fc), lambda b, c, be, nu: (be[live(b, nu)], 0, chunk(b, c, nu))),
                  pl.BlockSpec((1, fc, d), lambda b, c, be, nu: (be[live(b, nu)], chunk(b, c, nu), 0))],
        out_specs=pl.BlockSpec((rb, d), lambda b, c, be, nu: (live(b, nu), 0)),
        scratch_shapes=[pltpu.VMEM((rb, d), F32)],
    )
    return pl.pallas_call(
        functools.partial(_moe_ffn_kernel, nch=nch, subs=subs),
        grid_spec=grid_spec,
        out_shape=jax.ShapeDtypeStruct((r, d), BF16),
        compiler_params=_cparams(("arbitrary", "arbitrary"), MOE_VMEM_LIMIT),
        name="moe_ffn",
    )(block_expert, n_used, xs, wg, wu, wd)


def _gather_rows(x, idx):
    return x.at[idx].get(mode="promise_in_bounds")


def _moe(hn, eg, wg, wu, wd, *, rb, n_experts, moe_layer):
    n, d = hn.shape
    rows = TOP_K * n
    flat_e = eg[:, TOP_K:2 * TOP_K].astype(jnp.int32).T.reshape(-1)
    onehot = (jnp.arange(n_experts, dtype=jnp.int32)[:, None] == flat_e[None, :]).astype(jnp.int32)
    csum = jnp.cumsum(onehot, axis=1)
    count = csum[:, -1]
    first_sorted = jnp.cumsum(count) - count
    padded = ((count + rb - 1) // rb) * rb
    ends = jnp.cumsum(padded)
    offset = ends - padded
    pos = jnp.sum(onehot * (csum - 1 + offset[:, None]), axis=0)
    nblk = pl.cdiv(rows, rb) + n_experts
    r_tot = nblk * rb
    block_start = jnp.arange(nblk, dtype=jnp.int32) * rb
    block_expert = jnp.minimum(jnp.sum((block_start[:, None] >= ends[None, :]).astype(jnp.int32), axis=1),
                               n_experts - 1).astype(jnp.int32)
    n_used = (ends[-1:] // rb).astype(jnp.int32)
    order = jnp.argsort(flat_e, stable=True).astype(jnp.int32)
    slot_e = jnp.repeat(block_expert, rb)
    local = jnp.arange(r_tot, dtype=jnp.int32) - offset[slot_e]
    src_row = order[jnp.clip(first_sorted[slot_e] + local, 0, rows - 1)]
    xs = _gather_rows(hn, src_row % n)
    ys = _moe_ffn(xs, block_expert + moe_layer * n_experts, n_used, wg, wu, wd, rb=rb)
    return _gather_rows(ys, pos)


def _ple_kernel(*refs, with_moe, final):
    refs = list(refs)
    h_ref = refs.pop(0)
    y_refs = (refs.pop(0), refs.pop(0), refs.pop(0)) if with_moe else ()
    p_ref, g_ref, wg_ref, wp_ref = refs[:4]
    refs = refs[4:]
    gf_ref = refs.pop(0) if final else None
    o_ref = refs[0]
    h = h_ref[...]
    if with_moe:
        y1_ref, y2_ref, eg_ref = y_refs
        gates = eg_ref[...]
        h = h + gates[:, 0:1] * y1_ref[...].astype(F32) + gates[:, 1:2] * y2_ref[...].astype(F32)
    gate = jax.nn.sigmoid(jnp.dot(_rms(h, g_ref[...]).astype(BF16), wg_ref[...],
                                  preferred_element_type=F32))
    h = h + gate * jnp.dot(p_ref[0].astype(BF16), wp_ref[...], preferred_element_type=F32)
    if final:
        h = _rms(h, gf_ref[...])
    o_ref[...] = h


def _ple(h, y, eg, p_all, g, wg, wp, g_final=None, *, layer):
    n, d = h.shape
    tm = TOKEN_TILE if n % TOKEN_TILE == 0 else n
    nt = n // tm
    tile = lambda width: pl.BlockSpec((tm, width), lambda i: (i, 0))
    in_specs, args = [tile(d)], [h]
    if y is not None:
        in_specs += [tile(d), pl.BlockSpec((tm, d), lambda i: (i + nt, 0)), tile(LANES)]
        args += [y, y, eg]
    in_specs += [pl.BlockSpec((1, tm, p_all.shape[2]), lambda i: (layer, i, 0)),
                 _full((1, d)), _resident(wg.shape), _resident(wp.shape)]
    args += [p_all, g, wg, wp]
    if g_final is not None:
        in_specs.append(_full((1, d)))
        args.append(g_final)
    return pl.pallas_call(
        functools.partial(_ple_kernel, with_moe=y is not None, final=g_final is not None),
        grid=(nt,), in_specs=in_specs, out_specs=tile(d),
        out_shape=jax.ShapeDtypeStruct((n, d), F32),
        compiler_params=_cparams(("arbitrary",)),
        name="ple",
    )(*args)


def _prep_w_in(w_in, dims):
    faw, sw, rw, nh = dims["faw"], dims["sw"], dims["rw"], dims["heads"]
    sizes = (faw, faw, faw, nh, sw, sw, rw, rw, rw, rw)
    parts, o = [], 0
    for s in sizes:
        parts.append(w_in[..., o:o + s])
        o += s
    fa_q, fa_k, fa_v, fa_f, sgu_u, sgu_v, ret_q, ret_k, ret_v, ret_g = parts

    def rot_cols(w):
        lead = w.shape[:-1]
        wh = w.reshape(lead + (rw // HEAD_DIM, 2, HEAD_DIM // 2))
        return jnp.stack([-wh[..., 1, :], wh[..., 0, :]], axis=-2).reshape(lead + (rw,))

    f_pad = jnp.pad(fa_f, [(0, 0)] * (w_in.ndim - 1) + [(0, LANES - nh)])
    cols = [("q", fa_q), ("k", fa_k), ("v", fa_v), ("u", sgu_u), ("sv", sgu_v), ("rq", ret_q),
            ("rqr", rot_cols(ret_q)), ("rk", ret_k), ("rkr", rot_cols(ret_k)), ("rv", ret_v),
            ("rg", ret_g), ("f", f_pad)]
    offs, o = {}, 0
    for name, c in cols:
        offs[name] = (o, o + c.shape[-1])
        o += c.shape[-1]
    return jnp.concatenate([c for _, c in cols], axis=-1).astype(BF16), offs


def _rope_tables(pos, ret_heads):
    half = HEAD_DIM // 2
    inv = ROPE_BASE ** (-jnp.arange(half, dtype=F32) / half)
    ang = pos.astype(F32)[:, None] * inv[None, :]
    cos = jnp.tile(jnp.cos(ang), (1, 2 * ret_heads))
    sin = jnp.tile(jnp.sin(ang), (1, 2 * ret_heads))
    return cos, sin


def _pair_states(s):
    b, hds = s.shape[:2]
    s = s.reshape(b, hds // 2, 2, HEAD_DIM, HEAD_DIM)
    z = jnp.zeros_like(s[:, :, 0])
    top = jnp.concatenate([s[:, :, 0], z], axis=-1)
    bot = jnp.concatenate([z, s[:, :, 1]], axis=-1)
    return jnp.concatenate([top, bot], axis=-2)


def _unpair_states(sp):
    a = sp[:, :, :HEAD_DIM, :HEAD_DIM]
    b = sp[:, :, HEAD_DIM:, HEAD_DIM:]
    return jnp.stack([a, b], axis=2).reshape(sp.shape[0], -1, HEAD_DIM, HEAD_DIM)


def _trunk(x, p, cache, wts, dims):
    batch, t, d = x.shape
    n = batch * t
    depth = wts["w_in"].shape[0]
    heads, ret_heads = dims["heads"], dims["ret_heads"]
    has_past = cache is not None
    offset = cache[0].shape[2] if has_past else 0
    pos = offset + jnp.arange(t, dtype=jnp.int32)
    cos, sin = _rope_tables(pos, ret_heads)
    if t < TOKEN_TILE:
        cos, sin = jnp.tile(cos, (batch, 1)), jnp.tile(sin, (batch, 1))

    if has_past:
        ret_blk, ret_rows = LANES, LANES
        tabs = _retention_tables(ret_heads, t, ret_blk)
    else:
        ret_blk = RET_CHUNK if t % RET_CHUNK == 0 else t
        ret_rows = t
        tabs = _retention_tables(ret_heads, ret_blk, ret_blk)

    sgu_full = wts["sgu_w"].shape[2]
    sgu_len = sgu_full if t >= sgu_full else t

    h = x.reshape(n, d)
    p_all = p.reshape(depth, n, -1)
    new_lf, new_ret, new_sgu = [], [], []
    kbuf = jnp.zeros((depth, n * heads, HEAD_DIM), F32)
    vbuf = jnp.zeros((depth, n * heads, HEAD_DIM), F32)
    for i in range(depth):
        outs = _inproj(h, wts["g_mix"][i], wts["w_in"][i], wts["b_forget"][i], cos, sin,
                       wts["sgu_ln_g"][i], wts["sgu_ln_b"][i], kbuf, vbuf, layer=i,
                       offs=wts["offs"], dims=dims, seq_len=t, prompt=not has_past)
        if has_past:
            kbuf, vbuf, qb, kb, vb, lf, u, sv, rq, rk, rv, rg = outs
            oa = _fox_sample(qb, kb, vb, lf, cache[0][i].reshape(batch, offset, -1),
                             cache[1][i].reshape(batch, offset, -1), cache[2][i],
                             batch=batch, ts=t, heads=heads)
        else:
            kbuf, vbuf, qa, ka, vt, stats, lf, u, sv, rq, rk, rv, rg = outs
            oa = _fox_prompt(qa, ka, vt, stats, batch=batch, seq=t, heads=heads)

        sgu_w, sgu_b = wts["sgu_w"][i], wts["sgu_b"][i]
        groups = sgu_w.shape[0]
        w_l = sgu_w[:, :sgu_len, :sgu_len]
        b_l = jnp.repeat(sgu_b[:, :sgu_len].T, HEAD_DIM, axis=1)
        if sgu_len % 128 != 0:
            reps = n // sgu_len
            eye = jnp.eye(reps, dtype=F32)
            w_l = jnp.einsum("ab,gts->gatbs", eye, w_l).reshape(groups, n, n)
            b_l = jnp.tile(b_l, (reps, 1))
        ob = _sgu(u, sv, w_l, b_l)

        if has_past:
            padr = lambda a: jnp.pad(a.reshape(batch, t, -1), ((0, 0), (0, ret_rows - t), (0, 0))
                                     ).reshape(batch * ret_rows, -1)
            s0 = _pair_states(cache[3][i].astype(F32))
            oc, sfin = _retention(padr(rq), padr(rk), padr(rv), padr(rg), s0, tabs, wts["ret_gn_g"][i],
                                  batch=batch, rows_per_seq=ret_rows, blk=ret_blk)
            oc = oc.reshape(batch, ret_rows, -1)[:, :t].reshape(n, -1)
        else:
            s0 = jnp.zeros((batch, ret_heads // 2, LANES, LANES), F32)
            oc, sfin = _retention(rq, rk, rv, rg, s0, tabs, wts["ret_gn_g"][i],
                                  batch=batch, rows_per_seq=ret_rows, blk=ret_blk)

        j = i // 2
        is_moe = i % 2 == 1
        last = i == depth - 1
        g_final = wts["g_final"] if last else None
        if is_moe:
            router = (wts["w_router"][j], wts["b_router"][j], wts["n_experts"])
            h, hn, eg = _outproj(h, oa, ob, oc, wts["w_out"][i], wts["g_ffn"][i], router)
            rb = MOE_ROW_BLOCK if n >= 8 * MOE_ROW_BLOCK else LANES
            y = _moe(hn, eg, wts["w_exp_gate"], wts["w_exp_up"], wts["w_exp_down"], rb=rb,
                     n_experts=wts["n_experts"], moe_layer=j)
            h = _ple(h, y, eg, p_all, wts["g_ple"][i], wts["w_ple_gate"][i],
                     wts["w_ple_proj"][i], g_final, layer=i)
        else:
            h = _dense_tail(h, oa, ob, oc, wts["w_out"][i], wts["g_ffn"][i], wts["w_dense_gate"][j],
                            wts["w_dense_up"][j], wts["w_dense_down"][j], p_all, wts["g_ple"][i],
                            wts["w_ple_gate"][i], wts["w_ple_proj"][i], g_final, layer=i)

        new_lf.append(lf.reshape(batch, t, heads))
        new_ret.append(_unpair_states(sfin))
        new_sgu.append(sv.reshape(batch, t, -1))

    y = h.reshape(batch, t, d)
    kv_shape = (depth, batch, t, heads, HEAD_DIM)
    return (y, kbuf.reshape(kv_shape), vbuf.reshape(kv_shape), jnp.stack(new_lf), jnp.stack(new_ret),
            jnp.stack(new_sgu))


def kernel(x_prompt, x_sample, cache_fa_k, cache_fa_v, cache_fa_logf, state_ret, p_prompt, p_sample, g_mix, w_in, b_forget, sgu_ln_g, sgu_ln_b, sgu_w, sgu_b, ret_gn_g, w_out, g_ffn, w_dense_gate, w_dense_up, w_dense_down, w_router, b_router, w_exp_gate, w_exp_up, w_exp_down, g_ple, w_ple_gate, w_ple_proj, g_final):
    heads = cache_fa_k.shape[3]
    ret_heads = state_ret.shape[2]
    groups = sgu_w.shape[1]
    n_experts = w_router.shape[-1]
    dims = dict(heads=heads, ret_heads=ret_heads, faw=heads * HEAD_DIM, sw=groups * HEAD_DIM,
                rw=ret_heads * HEAD_DIM)
    assert cache_fa_k.shape[4] == HEAD_DIM and heads % 2 == 0 and ret_heads % 2 == 0

    w_in_b, offs = _prep_w_in(w_in, dims)
    row = lambda a: a[:, None, :].astype(F32)
    wr1 = w_router.astype(BF16)
    wr2 = (w_router - wr1.astype(F32)).astype(BF16)
    pad_e = lambda a: jnp.pad(a, [(0, 0)] * (a.ndim - 1) + [(0, LANES - n_experts)])
    wts = dict(
        offs=offs, n_experts=n_experts,
        g_mix=row(g_mix), w_in=w_in_b,
        b_forget=row(jnp.pad(b_forget, ((0, 0), (0, LANES - heads)))),
        sgu_ln_g=row(sgu_ln_g), sgu_ln_b=row(sgu_ln_b), sgu_w=sgu_w, sgu_b=sgu_b,
        ret_gn_g=row(ret_gn_g), w_out=w_out.astype(BF16), g_ffn=row(g_ffn),
        w_dense_gate=w_dense_gate.astype(BF16), w_dense_up=w_dense_up.astype(BF16),
        w_dense_down=w_dense_down.astype(BF16),
        w_router=jnp.concatenate([pad_e(wr1), pad_e(wr2)], axis=-1), b_router=row(pad_e(b_router)),
        w_exp_gate=w_exp_gate.astype(BF16).reshape((-1,) + w_exp_gate.shape[2:]),
        w_exp_up=w_exp_up.astype(BF16).reshape((-1,) + w_exp_up.shape[2:]),
        w_exp_down=w_exp_down.astype(BF16).reshape((-1,) + w_exp_down.shape[2:]),
        g_ple=row(g_ple), w_ple_gate=w_ple_gate.astype(BF16), w_ple_proj=w_ple_proj.astype(BF16),
        g_final=g_final[None, :].astype(F32),
    )

    y_p, k_p, v_p, lf_p, ret_p, _ = _trunk(x_prompt, p_prompt, None, wts, dims)
    y_s, k_s, v_s, lf_s, ret_s, sgu_s = _trunk(
        x_sample, p_sample, (cache_fa_k, cache_fa_v, cache_fa_logf, state_ret), wts, dims)
    return (y_p, y_s, k_p, v_p, lf_p, ret_p, k_s, v_s, lf_s, ret_s, sgu_s)
```

```python
import functools
import math

import jax
import jax.numpy as jnp
import numpy as np
from jax import lax
from jax.experimental import pallas as pl
from jax.experimental.pallas import tpu as pltpu

F32 = jnp.float32
BF16 = jnp.bfloat16

HEAD_DIM = 64
LANES = 128
MXU_DEPTH = 256
ROPE_BASE = 10000.0
RET_DECAY_MIN = 1.0 / 32.0
RET_DECAY_MAX = 1.0 / 512.0
NORM_EPS = 1e-6
GN_EPS = 1e-5
NEG_INF = -1e30
TOP_K = 2
LOG2E = 1.4426950408889634

TOKEN_TILE = 512
ATTN_TILE = 512
RET_CHUNK = 512
MOE_ROW_BLOCK = 1024
VMEM_LIMIT = 48 * 1024 * 1024
MOE_VMEM_LIMIT = 56 * 1024 * 1024


def _cparams(sem, vmem=VMEM_LIMIT):
    return pltpu.CompilerParams(dimension_semantics=sem, vmem_limit_bytes=vmem)


def _full(shape):
    zeros = (0,) * len(shape)
    return pl.BlockSpec(shape, lambda *_: zeros)


def _resident(shape):
    zeros = (0,) * len(shape)
    return pl.BlockSpec(shape, lambda *_: zeros, pipeline_mode=pl.Buffered(1))


def _rms(x, g):
    return x * lax.rsqrt(jnp.mean(x * x, axis=-1, keepdims=True) + NORM_EPS) * g


def _silu(x):
    return x * (1.0 / (1.0 + jnp.exp(-x)))


def _split3(x):
    p1 = x.astype(BF16)
    r1 = x - p1.astype(F32)
    p2 = r1.astype(BF16)
    p3 = (r1 - p2.astype(F32)).astype(BF16)
    return p1, p2, p3


def _nt_dot(a, b):
    return lax.dot_general(a, b, (((1,), (1,)), ((), ())), preferred_element_type=F32)


def _inproj_kernel(*refs, offs, tiles_per_seq, prompt, tm, heads):
    (h_ref, g_ref, w_ref, bf_ref, cos_ref, sin_ref, lng_ref, lnb_ref) = refs[:8]
    if prompt:
        (selk_ref, onek_ref, selq_ref, oneq_ref, hsel_ref) = refs[10:15]
        (k32_ref, v32_ref, qa_ref, ka_ref, vt_ref, st_ref, lf_ref, u_ref, sv_ref,
         rq_ref, rk_ref, rv_ref, rg_ref, carry_ref) = refs[15:]
    else:
        (k32_ref, v32_ref, q_ref, kb_ref, vb_ref, lf_ref, u_ref, sv_ref,
         rq_ref, rk_ref, rv_ref, rg_ref) = refs[10:]

    xn = _rms(h_ref[...], g_ref[...]).astype(BF16)

    def proj(name):
        a, b = offs[name]
        return jnp.dot(xn, w_ref[:, a:b], preferred_element_type=F32)

    f = proj("f") + bf_ref[...]
    lf = jnp.minimum(f, 0.0) - jnp.log1p(jnp.exp(-jnp.abs(f)))
    nh = lf_ref.shape[-1]
    lf_ref[...] = lf[:, :nh]

    if prompt:
        i = pl.program_id(0)

        @pl.when(i % tiles_per_seq == 0)
        def _():
            carry_ref[...] = jnp.zeros_like(carry_ref)

        row = lax.broadcasted_iota(jnp.int32, (tm, tm), 0)
        col = lax.broadcasted_iota(jnp.int32, (tm, tm), 1)
        tril = jnp.where(col <= row, 1.0, 0.0).astype(BF16)
        pieces = jnp.concatenate(_split3(lf), axis=1)
        cs = jnp.dot(tril, pieces, preferred_element_type=F32)
        fc = cs[:, :LANES] + cs[:, LANES:2 * LANES] + cs[:, 2 * LANES:] + carry_ref[...]
        carry_ref[...] = fc[tm - 1:tm, :]

        fpieces = jnp.concatenate(_split3(fc * LOG2E), axis=1)
        kaug = jnp.dot(fpieces, selk_ref[...], preferred_element_type=F32) + onek_ref[...]
        qaug = jnp.dot(fpieces, selq_ref[...], preferred_element_type=F32) + oneq_ref[...]
        for p in range(heads // 2):
            ka_ref[p, :, LANES:] = kaug[:, p * LANES:(p + 1) * LANES].astype(BF16)
        for hd in range(heads):
            qa_ref[hd, :, LANES:] = qaug[:, hd * LANES:(hd + 1) * LANES].astype(BF16)

    q = proj("q") * (HEAD_DIM ** -0.5)
    k = proj("k")
    v = proj("v")
    for hd in range(heads):
        hs = slice(hd * HEAD_DIM, (hd + 1) * HEAD_DIM)
        k32_ref[0, pl.ds(hd, tm, stride=heads), :] = k[:, hs]
        v32_ref[0, pl.ds(hd, tm, stride=heads), :] = v[:, hs]
    if prompt:
        sq = jnp.concatenate([q * q, k * k], axis=1) * (1.0 + 2.0 ** -7)
        n2 = jnp.max(jnp.dot(sq.astype(BF16), hsel_ref[...], preferred_element_type=F32),
                     axis=0, keepdims=True)
        st_ref[0] = jnp.concatenate([n2[:, :LANES], n2[:, LANES:], fc[0:1, :], fc[tm - 1:tm, :],
                                     jnp.zeros((4, LANES), F32)], axis=0)

        lane_head = lax.broadcasted_iota(jnp.int32, (1, LANES), 1) // HEAD_DIM
        q2 = q * LOG2E
        for p in range(heads // 2):
            sl = slice(p * LANES, (p + 1) * LANES)
            ka_ref[p, :, :LANES] = k[:, sl].astype(BF16)
            for hh in range(2):
                hd = 2 * p + hh
                own = lane_head == hh
                qa_ref[hd, :, :LANES] = jnp.where(own, q2[:, sl], 0.0).astype(BF16)
                vt_ref[hd, 0] = jnp.where(own, v[:, sl], 1.0).T.astype(BF16)
    else:
        q_ref[...] = q.astype(BF16)
        kb_ref[...] = k.astype(BF16)
        vb_ref[...] = v.astype(BF16)

    u_ref[...] = jax.nn.gelu(proj("u"))
    sv = jax.nn.gelu(proj("sv"))
    mu = jnp.mean(sv, axis=-1, keepdims=True)
    var = jnp.mean(jnp.square(sv - mu), axis=-1, keepdims=True)
    sv_ref[...] = (sv - mu) * lax.rsqrt(var + GN_EPS) * lng_ref[...] + lnb_ref[...]

    cos = cos_ref[...]
    sin = sin_ref[...]
    rq_ref[...] = (proj("rq") * cos + proj("rqr") * sin).astype(BF16)
    rk_ref[...] = ((proj("rk") * cos + proj("rkr") * sin) * (HEAD_DIM ** -0.5)).astype(BF16)
    rv_ref[...] = proj("rv").astype(BF16)
    rg_ref[...] = _silu(proj("rg")).astype(BF16)


def _bias_selectors(heads):
    npairs = heads // 2
    selk = np.zeros((3 * LANES, npairs * LANES), np.float32)
    onek = np.zeros((1, npairs * LANES), np.float32)
    selq = np.zeros((3 * LANES, heads * LANES), np.float32)
    oneq = np.zeros((1, heads * LANES), np.float32)
    for p in range(npairs):
        for a in range(3):
            for hh in range(2):
                selk[a * LANES + 2 * p + hh, p * LANES + 3 * hh + a] = -1.0
            onek[0, p * LANES + 6 + a] = 1.0
    for hd in range(heads):
        for a in range(3):
            selq[a * LANES + hd, hd * LANES + 6 + a] = 1.0
            oneq[0, hd * LANES + 3 * (hd % 2) + a] = 1.0
    faw = heads * HEAD_DIM
    col = np.concatenate([np.arange(faw) // HEAD_DIM, LANES + np.arange(faw) // HEAD_DIM])
    hsel = col[:, None] == np.arange(2 * LANES)[None, :]
    return (jnp.asarray(selk, BF16), jnp.asarray(onek), jnp.asarray(selq, BF16), jnp.asarray(oneq),
            jnp.asarray(hsel, BF16))


def _inproj(h, g, w, bf, cos, sin, lng, lnb, kbuf, vbuf, *, layer, offs, dims, seq_len, prompt):
    n, d = h.shape
    tm = TOKEN_TILE if n % TOKEN_TILE == 0 else n
    nt = n // tm
    tiles_per_seq = max(seq_len // tm, 1)
    tab_blocks = cos.shape[0] // tm
    faw, sw, rw, nh = dims["faw"], dims["sw"], dims["rw"], dims["heads"]
    npairs = nh // 2
    tile = lambda width: pl.BlockSpec((tm, width), lambda i: (i, 0))
    tab = pl.BlockSpec((tm, rw), lambda i: (i % tab_blocks, 0))
    hbm = pl.BlockSpec(memory_space=pl.ANY)
    in_specs = [tile(d), _full((1, d)), _resident(w.shape), _full((1, LANES)), tab, tab,
                _full((1, sw)), _full((1, sw)), hbm, hbm]
    args = [h, g, w, bf, cos, sin, lng, lnb, kbuf, vbuf]
    layer_rows = pl.BlockSpec((1, tm * nh, HEAD_DIM), lambda i: (layer, i, 0))
    out_shape = [jax.ShapeDtypeStruct(kbuf.shape, F32), jax.ShapeDtypeStruct(vbuf.shape, F32)]
    out_specs = [layer_rows, layer_rows]
    bf_rows = jax.ShapeDtypeStruct((n, faw), BF16)
    if prompt:
        assert tm == ATTN_TILE
        sels = _bias_selectors(nh)
        in_specs += [_full(s.shape) for s in sels]
        args += list(sels)
        out_shape += [jax.ShapeDtypeStruct((nh, n, 2 * LANES), BF16),
                      jax.ShapeDtypeStruct((npairs, n, 2 * LANES), BF16),
                      jax.ShapeDtypeStruct((nh, nt, LANES, tm), BF16),
                      jax.ShapeDtypeStruct((nt, 8, LANES), F32)]
        out_specs += [pl.BlockSpec((nh, tm, 2 * LANES), lambda i: (0, i, 0)),
                      pl.BlockSpec((npairs, tm, 2 * LANES), lambda i: (0, i, 0)),
                      pl.BlockSpec((nh, 1, LANES, tm), lambda i: (0, i, 0, 0)),
                      pl.BlockSpec((1, 8, LANES), lambda i: (i, 0, 0))]
    else:
        out_shape += [bf_rows, bf_rows, bf_rows]
        out_specs += [tile(faw)] * 3
    out_shape += [jax.ShapeDtypeStruct((n, nh), F32),
                  jax.ShapeDtypeStruct((n, sw), F32), jax.ShapeDtypeStruct((n, sw), F32)]
    out_specs += [tile(nh), tile(sw), tile(sw)]
    out_shape += [jax.ShapeDtypeStruct((n, rw), BF16)] * 4
    out_specs += [tile(rw)] * 4
    scratch = [pltpu.VMEM((1, LANES), F32)] if prompt else []
    return pl.pallas_call(
        functools.partial(_inproj_kernel, offs=offs, tiles_per_seq=tiles_per_seq,
                          prompt=prompt, tm=tm, heads=nh),
        grid=(nt,), in_specs=in_specs, out_specs=out_specs, out_shape=out_shape,
        scratch_shapes=scratch, input_output_aliases={8: 0, 9: 1},
        compiler_params=_cparams(("arbitrary",)),
        name="inproj",
    )(*args)


def _fox_prompt_kernel(first_ref, fast_ref, q_ref, k_ref, vt_ref, o_ref, m_ref, acc_ref, *, t):
    qi = pl.program_id(2)
    idx = (pl.program_id(0) * pl.num_programs(1) + pl.program_id(1)) * pl.num_programs(2) + qi
    first = first_ref[idx]
    acc_ref[...] = jnp.zeros(acc_ref.shape, F32)

    def scores(ki, hh, masked):
        start = pl.multiple_of(ki * t, t)
        s = _nt_dot(k_ref[0, pl.ds(start, t), :], q_ref[hh])
        if masked:
            key = lax.broadcasted_iota(jnp.int32, (t, t), 0)
            qry = lax.broadcasted_iota(jnp.int32, (t, t), 1)
            s = jnp.where(key <= qry, s, NEG_INF)
        return s

    def plain_step(ki, masked):
        pr = [jnp.exp2(scores(ki, hh, masked)).astype(BF16) for hh in range(2)]
        for hh in range(2):
            acc_ref[hh] += jnp.dot(vt_ref[hh, ki], pr[hh], preferred_element_type=F32)

    def online_step(ki, masked):
        for hh in range(2):
            s = scores(ki, hh, masked)
            m_prev = m_ref[hh]
            m_new = jnp.maximum(m_prev, jnp.max(s, axis=0, keepdims=True))
            pr = jnp.exp2(s - m_new).astype(BF16)
            acc_ref[hh] = (jnp.exp2(m_prev - m_new) * acc_ref[hh]
                           + jnp.dot(vt_ref[hh, ki], pr, preferred_element_type=F32))
            m_ref[hh] = m_new

    def run(step):
        def body(ki, carry):
            step(ki, False)
            return carry

        lax.fori_loop(first, qi, body, 0)
        step(qi, True)

    def plain_pair(ki):
        start = pl.multiple_of(ki * t, t)
        ks = k_ref[0, pl.ds(start, 2 * t), :]
        pr = [jnp.exp2(_nt_dot(ks, q_ref[hh])).astype(BF16) for hh in range(2)]
        for hh in range(2):
            vts = jnp.concatenate([vt_ref[hh, ki], vt_ref[hh, ki + 1]], axis=1)
            acc_ref[hh] += jnp.dot(vts, pr[hh], preferred_element_type=F32)

    def run_plain():
        past = qi - first

        def body(j, carry):
            plain_pair(first + 2 * j)
            return carry

        lax.fori_loop(0, past // 2, body, 0)

        @pl.when(past % 2 == 1)
        def _():
            plain_step(jnp.maximum(qi - 1, 0), False)

        plain_step(qi, True)

    bounded = fast_ref[idx] == 1

    @pl.when(bounded)
    def _():
        run_plain()

    @pl.when(jnp.logical_not(bounded))
    def _():
        m_ref[...] = jnp.full(m_ref.shape, NEG_INF, F32)
        run(online_step)

    acc_a = acc_ref[0]
    acc_b = acc_ref[1]
    head_a = lax.broadcasted_iota(jnp.int32, (LANES, 1), 0) < HEAD_DIM
    o_t = jnp.where(head_a, acc_a / acc_a[HEAD_DIM:HEAD_DIM + 1, :], acc_b / acc_b[0:1, :])
    o_ref[...] = o_t.T.astype(BF16)


F32_EXP_ZERO = -104.0
PLAIN_SCORE_LIMIT = 30.0


def _attention_plan(stats, *, batch, nq, heads):
    st = stats.reshape(batch, nq, 8, LANES)[..., :heads]
    grow = 1.0 + 2.0 ** -6
    qn = jnp.sqrt(st[:, :, 0]) * grow
    kn = jnp.sqrt(jnp.max(st[:, :, 1], axis=1, keepdims=True)) * grow
    u = qn * kn
    plain = jnp.all((u <= PLAIN_SCORE_LIMIT).reshape(batch, nq, heads // 2, 2), axis=-1)
    plain_h = jnp.repeat(plain, 2, axis=-1)
    reach = jnp.where(plain_h, u, 2.0 * u)
    f_first, f_last = st[:, :, 2], st[:, :, 3]
    bound = reach[:, :, None, :] + f_first[:, :, None, :] - f_last[:, None, :, :] + 1.0
    past = (jnp.arange(nq)[None, :] < jnp.arange(nq)[:, None])[None, :, :, None]
    dead = jnp.argmax(~(past & (bound < F32_EXP_ZERO)), axis=2).astype(jnp.int32)
    dead = jnp.min(dead.reshape(batch, nq, heads // 2, 2), axis=-1)
    flat = lambda a: a.transpose(0, 2, 1).reshape(-1).astype(jnp.int32)
    return flat(dead), flat(plain)


def _fox_prompt(qa, ka, vt, stats, *, batch, seq, heads):
    n = qa.shape[1]
    t = ATTN_TILE
    nq = seq // t
    npairs = heads // 2
    first, plain = _attention_plan(stats, batch=batch, nq=nq, heads=heads)
    grid_spec = pltpu.PrefetchScalarGridSpec(
        num_scalar_prefetch=2,
        grid=(batch, npairs, nq),
        in_specs=[pl.BlockSpec((2, t, 2 * LANES), lambda b, p, i, *_: (p, b * nq + i, 0)),
                  pl.BlockSpec((1, seq, 2 * LANES), lambda b, p, i, *_: (p, b, 0)),
                  pl.BlockSpec((2, nq, LANES, t), lambda b, p, i, *_: (p, b, 0, 0))],
        out_specs=pl.BlockSpec((t, LANES), lambda b, p, i, *_: (b * nq + i, p)),
        scratch_shapes=[pltpu.VMEM((2, 1, t), F32), pltpu.VMEM((2, LANES, t), F32)],
    )
    return pl.pallas_call(
        functools.partial(_fox_prompt_kernel, t=t),
        grid_spec=grid_spec,
        out_shape=jax.ShapeDtypeStruct((n, heads * HEAD_DIM), BF16),
        compiler_params=_cparams(("arbitrary", "arbitrary", "arbitrary")),
        name="fox_prompt",
    )(first, plain, qa, ka, vt)


def _fox_sample_kernel(q_ref, kn_ref, vn_ref, pk_ref, pv_ref, plft_ref, lf_ref, lft_ref, o_ref,
                       *, heads, ts, past):
    hi = lax.Precision.HIGHEST
    r_ = lax.broadcasted_iota(jnp.int32, (past, past), 0)
    c_ = lax.broadcasted_iota(jnp.int32, (past, past), 1)
    after = jnp.where(r_ > c_, 1.0, 0.0).astype(F32)
    g_row = jnp.dot(plft_ref[0], after, precision=hi, preferred_element_type=F32)
    rr = lax.broadcasted_iota(jnp.int32, (LANES, LANES), 0)
    cc = lax.broadcasted_iota(jnp.int32, (LANES, LANES), 1)
    tril = jnp.where(cc <= rr, 1.0, 0.0).astype(F32)
    triu = jnp.where(rr <= cc, 1.0, 0.0).astype(F32)
    c_col = jnp.dot(tril, lf_ref[0], precision=hi, preferred_element_type=F32)[:ts]
    c_row = jnp.dot(lft_ref[0], triu, precision=hi, preferred_element_type=F32)
    visible = (lax.broadcasted_iota(jnp.int32, (ts, LANES), 1)
               <= lax.broadcasted_iota(jnp.int32, (ts, LANES), 0))
    lane = lax.broadcasted_iota(jnp.int32, (1, LANES), 1)
    lo = lane < HEAD_DIM
    for p in range(heads // 2):
        sl = slice(p * LANES, (p + 1) * LANES)
        q2 = q_ref[:, sl]
        kp2 = pk_ref[0, :, sl].astype(BF16)
        vp2 = pv_ref[0, :, sl].astype(BF16)
        kn2 = kn_ref[:, sl]
        vn2 = vn_ref[:, sl]
        zero = jnp.zeros_like(q2)
        res = []
        for hh in range(2):
            h = 2 * p + hh
            qh = jnp.where(lo, q2, zero) if hh == 0 else jnp.where(lo, zero, q2)
            s_p = _nt_dot(qh, kp2) + c_col[:, h:h + 1] + g_row[h:h + 1, :]
            s_n = _nt_dot(qh, kn2) + c_col[:, h:h + 1] - c_row[h:h + 1, :]
            s_n = jnp.where(visible, s_n, NEG_INF)
            m = jnp.maximum(jnp.max(s_p, axis=1, keepdims=True), jnp.max(s_n, axis=1, keepdims=True))
            pp = jnp.exp(s_p - m)
            pn = jnp.exp(s_n - m)
            l = jnp.sum(pp, axis=1, keepdims=True) + jnp.sum(pn, axis=1, keepdims=True)
            o = (jnp.dot(pp.astype(BF16), vp2, preferred_element_type=F32)
                 + jnp.dot(pn.astype(BF16), vn2, preferred_element_type=F32))
            res.append(o / l)
        o_ref[:, sl] = jnp.where(lo, res[0], res[1]).astype(BF16)


def _fox_sample(qb, kb, vb, lf, past_k, past_v, past_lf, *, batch, ts, heads):
    n, faw = qb.shape
    past = past_k.shape[1]
    plft = past_lf.transpose(0, 2, 1)
    lf3 = lf.reshape(batch, ts, heads)
    lfp = jnp.pad(lf3, ((0, 0), (0, LANES - ts), (0, LANES - heads)))
    lftp = jnp.pad(lf3.transpose(0, 2, 1), ((0, 0), (0, 0), (0, LANES - ts)))
    padr = lambda a: jnp.pad(a.reshape(batch, ts, faw), ((0, 0), (0, LANES - ts), (0, 0))
                             ).reshape(batch * LANES, faw)
    row = lambda rows: pl.BlockSpec((rows, faw), lambda b: (b, 0))
    return pl.pallas_call(
        functools.partial(_fox_sample_kernel, heads=heads, ts=ts, past=past),
        grid=(batch,),
        in_specs=[row(ts), row(LANES), row(LANES),
                  pl.BlockSpec((1, past, faw), lambda b: (b, 0, 0)),
                  pl.BlockSpec((1, past, faw), lambda b: (b, 0, 0)),
                  pl.BlockSpec((1, heads, past), lambda b: (b, 0, 0)),
                  pl.BlockSpec((1, LANES, LANES), lambda b: (b, 0, 0)),
                  pl.BlockSpec((1, heads, LANES), lambda b: (b, 0, 0))],
        out_specs=row(ts),
        out_shape=jax.ShapeDtypeStruct((n, faw), BF16),
        compiler_params=_cparams(("arbitrary",)),
        name="fox_sample",
    )(qb, padr(kb), padr(vb), past_k, past_v, plft, lfp, lftp)


def _sgu_kernel(u_ref, v_ref, w_ref, b_ref, o_ref, *, chunk, chunks_per_tile, groups):
    width = u_ref.shape[-1]
    row = lax.broadcasted_iota(jnp.int32, (chunk, chunk), 0)
    col = lax.broadcasted_iota(jnp.int32, (chunk, chunk), 1)
    causal = col <= row
    grp = lax.broadcasted_iota(jnp.int32, (1, width), 1) // HEAD_DIM
    wm = [jnp.where(causal, w_ref[g], 0.0).astype(BF16) for g in range(groups)]
    bias = b_ref[...]
    for c in range(chunks_per_tile):
        sl = slice(c * chunk, (c + 1) * chunk)
        vc = v_ref[sl, :].astype(BF16)
        mixed = jnp.zeros((chunk, width), F32)
        for g in range(groups):
            mixed = jnp.where(grp == g, jnp.dot(wm[g], vc, preferred_element_type=F32), mixed)
        o_ref[sl, :] = (u_ref[sl, :] * (mixed + bias)).astype(BF16)


def _sgu(u, v, w, bias):
    n, width = u.shape
    groups, chunk, _ = w.shape
    tile = TOKEN_TILE if (n % TOKEN_TILE == 0 and TOKEN_TILE % chunk == 0) else chunk
    tok = pl.BlockSpec((tile, width), lambda i: (i, 0))
    return pl.pallas_call(
        functools.partial(_sgu_kernel, chunk=chunk, chunks_per_tile=tile // chunk, groups=groups),
        grid=(n // tile,),
        in_specs=[tok, tok, _full(w.shape), _full(bias.shape)],
        out_specs=tok,
        out_shape=jax.ShapeDtypeStruct((n, width), BF16),
        compiler_params=_cparams(("arbitrary",)),
        name="sgu",
    )(u, v, w, bias)


def _retention_kernel(q_ref, k_ref, v_ref, g_ref, s0_ref, dmat_ref, qdec_ref, kdec_ref, sdec_ref,
                      bmask_ref, avg_ref, gn_ref, o_ref, sfin_ref, s_ref, *, npairs):
    c = pl.program_id(1)

    @pl.when(c == 0)
    def _():
        s_ref[...] = s0_ref[0]

    lane = lax.broadcasted_iota(jnp.int32, (1, LANES), 1)
    lo = lane < HEAD_DIM
    avg = avg_ref[...]

    def group_mean(x):
        x1 = x.astype(BF16)
        x2 = (x - x1.astype(F32)).astype(BF16)
        return (jnp.dot(x1, avg, preferred_element_type=F32)
                + jnp.dot(x2, avg, preferred_element_type=F32))

    for p in range(npairs):
        sl = slice(p * LANES, (p + 1) * LANES)
        q2 = q_ref[:, sl]
        k2 = k_ref[:, sl]
        v2 = v_ref[:, sl]
        zero = jnp.zeros_like(q2)
        inner = []
        for hh in range(2):
            qh = jnp.where(lo, q2, zero) if hh == 0 else jnp.where(lo, zero, q2)
            sc = _nt_dot(qh, k2) * dmat_ref[2 * p + hh]
            inner.append(jnp.dot(sc.astype(BF16), v2, preferred_element_type=F32))
        state = s_ref[p]
        cross = jnp.dot(q2, state.astype(BF16), preferred_element_type=F32) * qdec_ref[:, sl]
        o = jnp.where(lo, inner[0], inner[1]) + cross
        kd_t = (k2.astype(F32) * kdec_ref[:, sl]).T.astype(BF16)
        s_ref[p] = state * sdec_ref[p] + jnp.dot(kd_t, v2, preferred_element_type=F32) * bmask_ref[...]
        mu = group_mean(o)
        d = o - mu
        var = group_mean(d * d)
        on = d * lax.rsqrt(var + GN_EPS)
        o_ref[:, sl] = (on * gn_ref[:, sl] * g_ref[:, sl].astype(F32)).astype(BF16)

    sfin_ref[0] = s_ref[...]


def _retention(rq, rk, rv, rg, s0, tabs, gn, *, batch, rows_per_seq, blk):
    n, rw = rq.shape
    npairs = rw // LANES
    nc = rows_per_seq // blk
    tok = pl.BlockSpec((blk, rw), lambda b, c: (b * nc + c, 0))
    st = pl.BlockSpec((1, npairs, LANES, LANES), lambda b, c: (b, 0, 0, 0))
    dmat, qdec, kdec, sdec, bmask, avg = tabs
    return pl.pallas_call(
        functools.partial(_retention_kernel, npairs=npairs),
        grid=(batch, nc),
        in_specs=[tok, tok, tok, tok, st, _full(dmat.shape), _full(qdec.shape), _full(kdec.shape),
                  _full(sdec.shape), _full(bmask.shape), _full(avg.shape), _full(gn.shape)],
        out_specs=[tok, st],
        out_shape=[jax.ShapeDtypeStruct((n, rw), BF16),
                   jax.ShapeDtypeStruct((batch, npairs, LANES, LANES), F32)],
        scratch_shapes=[pltpu.VMEM((npairs, LANES, LANES), F32)],
        compiler_params=_cparams(("arbitrary", "arbitrary")),
        name="retention",
    )(rq, rk, rv, rg, s0, dmat, qdec, kdec, sdec, bmask, avg, gn)


def _retention_tables(ret_heads, true_len, blk):
    log_g = jnp.log1p(-jnp.exp(jnp.linspace(math.log(RET_DECAY_MIN), math.log(RET_DECAY_MAX),
                                            ret_heads, dtype=F32)))
    n = jnp.arange(blk, dtype=F32)
    diff = n[:, None] - n[None, :]
    causal = diff >= 0
    dmat = jnp.where(causal[None], jnp.exp(jnp.where(causal, diff, 0.0)[None] * log_g[:, None, None]), 0.0)
    per_lane = jnp.repeat(log_g, HEAD_DIM)
    qdec = jnp.exp((n[:, None] + 1.0) * per_lane[None, :])
    kdec = jnp.exp((true_len - 1.0 - n)[:, None] * per_lane[None, :])
    npairs = ret_heads // 2
    lane_head = jnp.arange(LANES) // HEAD_DIM
    bmask = (lane_head[:, None] == lane_head[None, :]).astype(F32)
    sdec = jnp.exp(true_len * per_lane).reshape(npairs, LANES)[:, :, None] * bmask[None]
    avg = (bmask / HEAD_DIM).astype(BF16)
    return dmat, qdec, kdec, sdec, bmask, avg


def _outproj_kernel(h_ref, oa_ref, ob_ref, oc_ref, w_ref, g_ref, wr_ref, br_ref,
                    hnew_ref, hn_ref, eg_ref, *, n_experts, faw, sw):
    o = (jnp.dot(oa_ref[...], w_ref[:faw, :], preferred_element_type=F32)
         + jnp.dot(ob_ref[...], w_ref[faw:faw + sw, :], preferred_element_type=F32)
         + jnp.dot(oc_ref[...], w_ref[faw + sw:, :], preferred_element_type=F32))
    h = h_ref[...] + o
    hnew_ref[...] = h
    hn = _rms(h, g_ref[...])
    hn_ref[...] = hn.astype(BF16)
    x1 = hn.astype(BF16)
    x2 = (hn - x1.astype(F32)).astype(BF16)
    a = jnp.dot(x1, wr_ref[...], preferred_element_type=F32)
    b = jnp.dot(x2, wr_ref[:, :LANES], preferred_element_type=F32)
    logits = a[:, :LANES] + a[:, LANES:] + b + br_ref[...]
    lane = lax.broadcasted_iota(jnp.int32, logits.shape, 1)
    lanef = lane.astype(F32)
    logits = jnp.where(lane < n_experts, logits, -jnp.inf)
    m1 = jnp.max(logits, axis=1, keepdims=True)
    i1 = jnp.min(jnp.where(logits == m1, lanef, float(LANES)), axis=1, keepdims=True)
    rest = jnp.where(lanef == i1, -jnp.inf, logits)
    m2 = jnp.max(rest, axis=1, keepdims=True)
    i2 = jnp.min(jnp.where(rest == m2, lanef, float(LANES)), axis=1, keepdims=True)
    e2 = jnp.exp(m2 - m1)
    g1 = 1.0 / (1.0 + e2)
    g2 = e2 / (1.0 + e2)
    eg_ref[...] = jnp.where(lane == 0, g1, jnp.where(lane == 1, g2,
                            jnp.where(lane == 2, i1, jnp.where(lane == 3, i2, 0.0))))


def _outproj(h, oa, ob, oc, w, g, router):
    n, d = h.shape
    tm = TOKEN_TILE if n % TOKEN_TILE == 0 else n
    faw, sw, rw = oa.shape[1], ob.shape[1], oc.shape[1]
    wr, br, n_experts = router
    tile = lambda width: pl.BlockSpec((tm, width), lambda i: (i, 0))
    return pl.pallas_call(
        functools.partial(_outproj_kernel, n_experts=n_experts, faw=faw, sw=sw),
        grid=(n // tm,),
        in_specs=[tile(d), tile(faw), tile(sw), tile(rw), _resident(w.shape), _full((1, d)),
                  _full(wr.shape), _full(br.shape)],
        out_specs=[tile(d), tile(d), tile(LANES)],
        out_shape=[jax.ShapeDtypeStruct((n, d), F32), jax.ShapeDtypeStruct((n, d), BF16),
                   jax.ShapeDtypeStruct((n, LANES), F32)],
        compiler_params=_cparams(("arbitrary",)),
        name="outproj",
    )(h, oa, ob, oc, w, g, wr, br)


def _dense_tail_kernel(*refs, final, fc, nchunks, faw, sw):
    (h_ref, oa_ref, ob_ref, oc_ref, wo_ref, gffn_ref, wg_ref, wu_ref, wd_ref,
     p_ref, gple_ref, wpg_ref, wpp_ref) = refs[:13]
    rest = list(refs[13:])
    gf_ref = rest.pop(0) if final else None
    o_ref, acc_ref = rest
    o = (jnp.dot(oa_ref[...], wo_ref[:faw, :], preferred_element_type=F32)
         + jnp.dot(ob_ref[...], wo_ref[faw:faw + sw, :], preferred_element_type=F32)
         + jnp.dot(oc_ref[...], wo_ref[faw + sw:, :], preferred_element_type=F32))
    h = h_ref[...] + o
    x = _rms(h, gffn_ref[...]).astype(BF16)
    for c in range(nchunks):
        sl = slice(c * fc, (c + 1) * fc)
        g = jnp.dot(x, wg_ref[:, sl], preferred_element_type=F32)
        u = jnp.dot(x, wu_ref[:, sl], preferred_element_type=F32)
        a = (_silu(g) * u).astype(BF16)
        y = jnp.dot(a, wd_ref[sl, :], preferred_element_type=F32)
        if c == 0:
            acc_ref[...] = y
        else:
            acc_ref[...] += y
    h = h + acc_ref[...]
    gate = jax.nn.sigmoid(jnp.dot(_rms(h, gple_ref[...]).astype(BF16), wpg_ref[...],
                                  preferred_element_type=F32))
    h = h + gate * jnp.dot(p_ref[0].astype(BF16), wpp_ref[...], preferred_element_type=F32)
    if final:
        h = _rms(h, gf_ref[...])
    o_ref[...] = h


def _dense_tail(h, oa, ob, oc, wo, gffn, wg, wu, wd, p_all, gple, wpg, wpp, g_final=None, *, layer):
    n, d = h.shape
    ff = wg.shape[1]
    tm = TOKEN_TILE if n % TOKEN_TILE == 0 else n
    fc = MXU_DEPTH if ff % MXU_DEPTH == 0 else ff
    faw, sw, rw = oa.shape[1], ob.shape[1], oc.shape[1]
    tile = lambda width: pl.BlockSpec((tm, width), lambda i: (i, 0))
    in_specs = [tile(d), tile(faw), tile(sw), tile(rw), _resident(wo.shape), _full((1, d)),
                _resident(wg.shape), _resident(wu.shape), _resident(wd.shape),
                pl.BlockSpec((1, tm, p_all.shape[2]), lambda i: (layer, i, 0)), _full((1, d)),
                _resident(wpg.shape), _resident(wpp.shape)]
    args = [h, oa, ob, oc, wo, gffn, wg, wu, wd, p_all, gple, wpg, wpp]
    if g_final is not None:
        in_specs.append(_full((1, d)))
        args.append(g_final)
    return pl.pallas_call(
        functools.partial(_dense_tail_kernel, final=g_final is not None, fc=fc, nchunks=ff // fc,
                          faw=faw, sw=sw),
        grid=(n // tm,), in_specs=in_specs, out_specs=tile(d),
        out_shape=jax.ShapeDtypeStruct((n, d), F32),
        scratch_shapes=[pltpu.VMEM((tm, d), F32)],
        compiler_params=_cparams(("arbitrary",)),
        name="dense_tail",
    )(*args)


def _moe_ffn_kernel(be_ref, nu_ref, x_ref, wg_ref, wu_ref, wd_ref, o_ref, acc_ref, *, nch, subs):
    b = pl.program_id(0)
    c = pl.program_id(1)

    @pl.when(b >= nu_ref[0])
    def _():
        o_ref[...] = jnp.zeros_like(o_ref)

    @pl.when(b < nu_ref[0])
    def _():
        @pl.when(c == 0)
        def _():
            acc_ref[...] = jnp.zeros_like(acc_ref)

        x = x_ref[...]
        for a0, a1 in subs:
            g = jnp.dot(x, wg_ref[0, :, a0:a1], preferred_element_type=F32)
            u = jnp.dot(x, wu_ref[0, :, a0:a1], preferred_element_type=F32)
            a = (_silu(g) * u).astype(BF16)
            acc_ref[...] += jnp.dot(a, wd_ref[0, a0:a1, :], preferred_element_type=F32)

        @pl.when(c == nch - 1)
        def _():
            o_ref[...] = acc_ref[...].astype(BF16)


def _moe_ffn(xs, block_expert, n_used, wg, wu, wd, *, rb):
    r, d = xs.shape
    ff = wg.shape[2]
    nblk = r // rb
    nch = 2 if ff % (2 * MXU_DEPTH) == 0 else 1
    fc = ff // nch
    step = 2 * MXU_DEPTH
    subs = tuple((a0, min(a0 + step, fc)) for a0 in range(0, fc, step))

    def live(b, nu):
        return jnp.maximum(jnp.minimum(b, nu[0] - 1), 0)

    def chunk(b, c, nu):
        return jnp.where(b < nu[0], c, nch - 1)

    grid_spec = pltpu.PrefetchScalarGridSpec(
        num_scalar_prefetch=2,
        grid=(nblk, nch),
        in_specs=[pl.BlockSpec((rb, d), lambda b, c, be, nu: (live(b, nu), 0)),
                  pl.BlockSpec((1, d, fc), lambda b, c, be, nu: (be[live(b, nu)], 0, chunk(b, c, nu))),
                  pl.BlockSpec((1, d, fc), lambda b, c, be, nu: (be[live(b, nu)], 0, chunk(b, c, nu))),
                  pl.BlockSpec((1, fc, d), lambda b, c, be, nu: (be[live(b, nu)], chunk(b, c, nu), 0))],
        out_specs=pl.BlockSpec((rb, d), lambda b, c, be, nu: (b, 0)),
        scratch_shapes=[pltpu.VMEM((rb, d), F32)],
    )
    return pl.pallas_call(
        functools.partial(_moe_ffn_kernel, nch=nch, subs=subs),
        grid_spec=grid_spec,
        out_shape=jax.ShapeDtypeStruct((r, d), BF16),
        compiler_params=_cparams(("arbitrary", "arbitrary"), MOE_VMEM_LIMIT),
        name="moe_ffn",
    )(block_expert, n_used, xs, wg, wu, wd)


def _gather_rows(x, idx):
    return x.at[idx].get(mode="promise_in_bounds")


def _moe(hn, eg, wg, wu, wd, *, rb, n_experts, moe_layer):
    n, d = hn.shape
    rows = TOP_K * n
    flat_e = eg[:, TOP_K:2 * TOP_K].astype(jnp.int32).T.reshape(-1)
    onehot = (jnp.arange(n_experts, dtype=jnp.int32)[:, None] == flat_e[None, :]).astype(jnp.int32)
    csum = jnp.cumsum(onehot, axis=1)
    count = csum[:, -1]
    first_sorted = jnp.cumsum(count) - count
    padded = ((count + rb - 1) // rb) * rb
    ends = jnp.cumsum(padded)
    offset = ends - padded
    pos = jnp.sum(onehot * (csum - 1 + offset[:, None]), axis=0)
    nblk = pl.cdiv(rows, rb) + n_experts
    r_tot = nblk * rb
    block_start = jnp.arange(nblk, dtype=jnp.int32) * rb
    block_expert = jnp.minimum(jnp.sum((block_start[:, None] >= ends[None, :]).astype(jnp.int32), axis=1),
                               n_experts - 1).astype(jnp.int32)
    n_used = (ends[-1:] // rb).astype(jnp.int32)
    order = jnp.argsort(flat_e, stable=True).astype(jnp.int32)
    slot_e = jnp.repeat(block_expert, rb)
    local = jnp.arange(r_tot, dtype=jnp.int32) - offset[slot_e]
    src_row = order[jnp.clip(first_sorted[slot_e] + local, 0, rows - 1)]
    xs = _gather_rows(hn, src_row % n)
    ys = _moe_ffn(xs, block_expert + moe_layer * n_experts, n_used, wg, wu, wd, rb=rb)
    return _gather_rows(ys, pos)


def _ple_kernel(*refs, final):
    (h_ref, y1_ref, y2_ref, eg_ref, p_ref, g_ref, wg_ref, wp_ref) = refs[:8]
    gf_ref = refs[8] if final else None
    o_ref = refs[-1]
    gates = eg_ref[...]
    h = (h_ref[...] + gates[:, 0:1] * y1_ref[...].astype(F32)
         + gates[:, 1:2] * y2_ref[...].astype(F32))
    gate = jax.nn.sigmoid(jnp.dot(_rms(h, g_ref[...]).astype(BF16), wg_ref[...],
                                  preferred_element_type=F32))
    h = h + gate * jnp.dot(p_ref[0].astype(BF16), wp_ref[...], preferred_element_type=F32)
    if final:
        h = _rms(h, gf_ref[...])
    o_ref[...] = h


def _ple(h, y, eg, p_all, g, wg, wp, g_final=None, *, layer):
    n, d = h.shape
    tm = TOKEN_TILE if n % TOKEN_TILE == 0 else n
    nt = n // tm
    tile = lambda width: pl.BlockSpec((tm, width), lambda i: (i, 0))
    in_specs = [tile(d), tile(d), pl.BlockSpec((tm, d), lambda i: (i + nt, 0)), tile(LANES),
                pl.BlockSpec((1, tm, p_all.shape[2]), lambda i: (layer, i, 0)),
                _full((1, d)), _resident(wg.shape), _resident(wp.shape)]
    args = [h, y, y, eg, p_all, g, wg, wp]
    if g_final is not None:
        in_specs.append(_full((1, d)))
        args.append(g_final)
    return pl.pallas_call(
        functools.partial(_ple_kernel, final=g_final is not None),
        grid=(nt,), in_specs=in_specs, out_specs=tile(d),
        out_shape=jax.ShapeDtypeStruct((n, d), F32),
        compiler_params=_cparams(("arbitrary",)),
        name="ple",
    )(*args)


def _prep_w_in(w_in, dims):
    faw, sw, rw, nh = dims["faw"], dims["sw"], dims["rw"], dims["heads"]
    sizes = (faw, faw, faw, nh, sw, sw, rw, rw, rw, rw)
    parts, o = [], 0
    for s in sizes:
        parts.append(w_in[..., o:o + s])
        o += s
    fa_q, fa_k, fa_v, fa_f, sgu_u, sgu_v, ret_q, ret_k, ret_v, ret_g = parts

    def rot_cols(w):
        lead = w.shape[:-1]
        wh = w.reshape(lead + (rw // HEAD_DIM, 2, HEAD_DIM // 2))
        return jnp.stack([-wh[..., 1, :], wh[..., 0, :]], axis=-2).reshape(lead + (rw,))

    f_pad = jnp.pad(fa_f, [(0, 0)] * (w_in.ndim - 1) + [(0, LANES - nh)])
    cols = [("q", fa_q), ("k", fa_k), ("v", fa_v), ("u", sgu_u), ("sv", sgu_v), ("rq", ret_q),
            ("rqr", rot_cols(ret_q)), ("rk", ret_k), ("rkr", rot_cols(ret_k)), ("rv", ret_v),
            ("rg", ret_g), ("f", f_pad)]
    offs, o = {}, 0
    for name, c in cols:
        offs[name] = (o, o + c.shape[-1])
        o += c.shape[-1]
    return jnp.concatenate([c for _, c in cols], axis=-1).astype(BF16), offs


def _rope_tables(pos, ret_heads):
    half = HEAD_DIM // 2
    inv = ROPE_BASE ** (-jnp.arange(half, dtype=F32) / half)
    ang = pos.astype(F32)[:, None] * inv[None, :]
    cos = jnp.tile(jnp.cos(ang), (1, 2 * ret_heads))
    sin = jnp.tile(jnp.sin(ang), (1, 2 * ret_heads))
    return cos, sin


def _pair_states(s):
    b, hds = s.shape[:2]
    s = s.reshape(b, hds // 2, 2, HEAD_DIM, HEAD_DIM)
    z = jnp.zeros_like(s[:, :, 0])
    top = jnp.concatenate([s[:, :, 0], z], axis=-1)
    bot = jnp.concatenate([z, s[:, :, 1]], axis=-1)
    return jnp.concatenate([top, bot], axis=-2)


def _unpair_states(sp):
    a = sp[:, :, :HEAD_DIM, :HEAD_DIM]
    b = sp[:, :, HEAD_DIM:, HEAD_DIM:]
    return jnp.stack([a, b], axis=2).reshape(sp.shape[0], -1, HEAD_DIM, HEAD_DIM)


def _trunk(x, p, cache, wts, dims):
    batch, t, d = x.shape
    n = batch * t
    depth = wts["w_in"].shape[0]
    heads, ret_heads = dims["heads"], dims["ret_heads"]
    has_past = cache is not None
    offset = cache[0].shape[2] if has_past else 0
    pos = offset + jnp.arange(t, dtype=jnp.int32)
    cos, sin = _rope_tables(pos, ret_heads)
    if t < TOKEN_TILE:
        cos, sin = jnp.tile(cos, (batch, 1)), jnp.tile(sin, (batch, 1))

    if has_past:
        ret_blk, ret_rows = LANES, LANES
        tabs = _retention_tables(ret_heads, t, ret_blk)
    else:
        ret_blk = RET_CHUNK if t % RET_CHUNK == 0 else t
        ret_rows = t
        tabs = _retention_tables(ret_heads, ret_blk, ret_blk)

    sgu_full = wts["sgu_w"].shape[2]
    sgu_len = sgu_full if t >= sgu_full else t

    h = x.reshape(n, d)
    p_all = p.reshape(depth, n, -1)
    new_lf, new_ret, new_sgu = [], [], []
    kbuf = jnp.zeros((depth, n * heads, HEAD_DIM), F32)
    vbuf = jnp.zeros((depth, n * heads, HEAD_DIM), F32)
    for i in range(depth):
        outs = _inproj(h, wts["g_mix"][i], wts["w_in"][i], wts["b_forget"][i], cos, sin,
                       wts["sgu_ln_g"][i], wts["sgu_ln_b"][i], kbuf, vbuf, layer=i,
                       offs=wts["offs"], dims=dims, seq_len=t, prompt=not has_past)
        if has_past:
            kbuf, vbuf, qb, kb, vb, lf, u, sv, rq, rk, rv, rg = outs
            oa = _fox_sample(qb, kb, vb, lf, cache[0][i].reshape(batch, offset, -1),
                             cache[1][i].reshape(batch, offset, -1), cache[2][i],
                             batch=batch, ts=t, heads=heads)
        else:
            kbuf, vbuf, qa, ka, vt, stats, lf, u, sv, rq, rk, rv, rg = outs
            oa = _fox_prompt(qa, ka, vt, stats, batch=batch, seq=t, heads=heads)

        sgu_w, sgu_b = wts["sgu_w"][i], wts["sgu_b"][i]
        groups = sgu_w.shape[0]
        w_l = sgu_w[:, :sgu_len, :sgu_len]
        b_l = jnp.repeat(sgu_b[:, :sgu_len].T, HEAD_DIM, axis=1)
        if sgu_len % 128 != 0:
            reps = n // sgu_len
            eye = jnp.eye(reps, dtype=F32)
            w_l = jnp.einsum("ab,gts->gatbs", eye, w_l).reshape(groups, n, n)
            b_l = jnp.tile(b_l, (reps, 1))
        ob = _sgu(u, sv, w_l, b_l)

        if has_past:
            padr = lambda a: jnp.pad(a.reshape(batch, t, -1), ((0, 0), (0, ret_rows - t), (0, 0))
                                     ).reshape(batch * ret_rows, -1)
            s0 = _pair_states(cache[3][i].astype(F32))
            oc, sfin = _retention(padr(rq), padr(rk), padr(rv), padr(rg), s0, tabs, wts["ret_gn_g"][i],
                                  batch=batch, rows_per_seq=ret_rows, blk=ret_blk)
            oc = oc.reshape(batch, ret_rows, -1)[:, :t].reshape(n, -1)
        else:
            s0 = jnp.zeros((batch, ret_heads // 2, LANES, LANES), F32)
            oc, sfin = _retention(rq, rk, rv, rg, s0, tabs, wts["ret_gn_g"][i],
                                  batch=batch, rows_per_seq=ret_rows, blk=ret_blk)

        j = i // 2
        is_moe = i % 2 == 1
        last = i == depth - 1
        g_final = wts["g_final"] if last else None
        if is_moe:
            router = (wts["w_router"][j], wts["b_router"][j], wts["n_experts"])
            h, hn, eg = _outproj(h, oa, ob, oc, wts["w_out"][i], wts["g_ffn"][i], router)
            rb = MOE_ROW_BLOCK if n >= 8 * MOE_ROW_BLOCK else LANES
            y = _moe(hn, eg, wts["w_exp_gate"], wts["w_exp_up"], wts["w_exp_down"], rb=rb,
                     n_experts=wts["n_experts"], moe_layer=j)
            h = _ple(h, y, eg, p_all, wts["g_ple"][i], wts["w_ple_gate"][i],
                     wts["w_ple_proj"][i], g_final, layer=i)
        else:
            h = _dense_tail(h, oa, ob, oc, wts["w_out"][i], wts["g_ffn"][i], wts["w_dense_gate"][j],
                            wts["w_dense_up"][j], wts["w_dense_down"][j], p_all, wts["g_ple"][i],
                            wts["w_ple_gate"][i], wts["w_ple_proj"][i], g_final, layer=i)

        new_lf.append(lf.reshape(batch, t, heads))
        new_ret.append(_unpair_states(sfin))
        new_sgu.append(sv.reshape(batch, t, -1))

    y = h.reshape(batch, t, d)
    kv_shape = (depth, batch, t, heads, HEAD_DIM)
    return (y, kbuf.reshape(kv_shape), vbuf.reshape(kv_shape), jnp.stack(new_lf), jnp.stack(new_ret),
            jnp.stack(new_sgu))


def kernel(x_prompt, x_sample, cache_fa_k, cache_fa_v, cache_fa_logf, state_ret, p_prompt, p_sample, g_mix, w_in, b_forget, sgu_ln_g, sgu_ln_b, sgu_w, sgu_b, ret_gn_g, w_out, g_ffn, w_dense_gate, w_dense_up, w_dense_down, w_router, b_router, w_exp_gate, w_exp_up, w_exp_down, g_ple, w_ple_gate, w_ple_proj, g_final):
    heads = cache_fa_k.shape[3]
    ret_heads = state_ret.shape[2]
    groups = sgu_w.shape[1]
    n_experts = w_router.shape[-1]
    dims = dict(heads=heads, ret_heads=ret_heads, faw=heads * HEAD_DIM, sw=groups * HEAD_DIM,
                rw=ret_heads * HEAD_DIM)
    assert cache_fa_k.shape[4] == HEAD_DIM and heads % 2 == 0 and ret_heads % 2 == 0

    w_in_b, offs = _prep_w_in(w_in, dims)
    row = lambda a: a[:, None, :].astype(F32)
    wr1 = w_router.astype(BF16)
    wr2 = (w_router - wr1.astype(F32)).astype(BF16)
    pad_e = lambda a: jnp.pad(a, [(0, 0)] * (a.ndim - 1) + [(0, LANES - n_experts)])
    wts = dict(
        offs=offs, n_experts=n_experts,
        g_mix=row(g_mix), w_in=w_in_b,
        b_forget=row(jnp.pad(b_forget, ((0, 0), (0, LANES - heads)))),
        sgu_ln_g=row(sgu_ln_g), sgu_ln_b=row(sgu_ln_b), sgu_w=sgu_w, sgu_b=sgu_b,
        ret_gn_g=row(ret_gn_g), w_out=w_out.astype(BF16), g_ffn=row(g_ffn),
        w_dense_gate=w_dense_gate.astype(BF16), w_dense_up=w_dense_up.astype(BF16),
        w_dense_down=w_dense_down.astype(BF16),
        w_router=jnp.concatenate([pad_e(wr1), pad_e(wr2)], axis=-1), b_router=row(pad_e(b_router)),
        w_exp_gate=w_exp_gate.astype(BF16).reshape((-1,) + w_exp_gate.shape[2:]),
        w_exp_up=w_exp_up.astype(BF16).reshape((-1,) + w_exp_up.shape[2:]),
        w_exp_down=w_exp_down.astype(BF16).reshape((-1,) + w_exp_down.shape[2:]),
        g_ple=row(g_ple), w_ple_gate=w_ple_gate.astype(BF16), w_ple_proj=w_ple_proj.astype(BF16),
        g_final=g_final[None, :].astype(F32),
    )

    y_p, k_p, v_p, lf_p, ret_p, _ = _trunk(x_prompt, p_prompt, None, wts, dims)
    y_s, k_s, v_s, lf_s, ret_s, sgu_s = _trunk(
        x_sample, p_sample, (cache_fa_k, cache_fa_v, cache_fa_logf, state_ret), wts, dims)
    return (y_p, y_s, k_p, v_p, lf_p, ret_p, k_s, v_s, lf_s, ret_s, sgu_s)
```

```python
import functools
import math

import jax
import jax.numpy as jnp
import numpy as np
from jax import lax
from jax.experimental import pallas as pl
from jax.experimental.pallas import tpu as pltpu

F32 = jnp.float32
BF16 = jnp.bfloat16

HEAD_DIM = 64
LANES = 128
MXU_DEPTH = 256
ROPE_BASE = 10000.0
RET_DECAY_MIN = 1.0 / 32.0
RET_DECAY_MAX = 1.0 / 512.0
NORM_EPS = 1e-6
GN_EPS = 1e-5
NEG_INF = -1e30
TOP_K = 2
LOG2E = 1.4426950408889634

TOKEN_TILE = 512
ATTN_TILE = 512
RET_CHUNK = 512
MOE_ROW_BLOCK = 1024
VMEM_LIMIT = 48 * 1024 * 1024
MOE_VMEM_LIMIT = 56 * 1024 * 1024


def _cparams(sem, vmem=VMEM_LIMIT):
    return pltpu.CompilerParams(dimension_semantics=sem, vmem_limit_bytes=vmem)


def _full(shape):
    zeros = (0,) * len(shape)
    return pl.BlockSpec(shape, lambda *_: zeros)


def _resident(shape):
    zeros = (0,) * len(shape)
    return pl.BlockSpec(shape, lambda *_: zeros, pipeline_mode=pl.Buffered(1))


def _rms(x, g):
    return x * lax.rsqrt(jnp.mean(x * x, axis=-1, keepdims=True) + NORM_EPS) * g


def _silu(x):
    return x * (1.0 / (1.0 + jnp.exp(-x)))


def _split3(x):
    p1 = x.astype(BF16)
    r1 = x - p1.astype(F32)
    p2 = r1.astype(BF16)
    p3 = (r1 - p2.astype(F32)).astype(BF16)
    return p1, p2, p3


def _nt_dot(a, b):
    return lax.dot_general(a, b, (((1,), (1,)), ((), ())), preferred_element_type=F32)


def _inproj_kernel(*refs, offs, tiles_per_seq, prompt, tm, heads):
    (h_ref, g_ref, w_ref, bf_ref, cos_ref, sin_ref, lng_ref, lnb_ref) = refs[:8]
    if prompt:
        (selk_ref, onek_ref, selq_ref, oneq_ref, hsel_ref, sgw_ref, sgb_ref) = refs[10:17]
        (k32_ref, v32_ref, qa_ref, ka_ref, vt_ref, st_ref, lf_ref, ob_ref,
         rq_ref, rk_ref, rv_ref, rg_ref, carry_ref) = refs[17:]
    else:
        (k32_ref, v32_ref, q_ref, kb_ref, vb_ref, lf_ref, u_ref, sv_ref,
         rq_ref, rk_ref, rv_ref, rg_ref) = refs[10:]

    xn = _rms(h_ref[...], g_ref[...]).astype(BF16)

    def proj(name):
        a, b = offs[name]
        return jnp.dot(xn, w_ref[:, a:b], preferred_element_type=F32)

    f = proj("f") + bf_ref[...]
    lf = jnp.minimum(f, 0.0) - jnp.log1p(jnp.exp(-jnp.abs(f)))
    nh = lf_ref.shape[-1]
    lf_ref[...] = lf[:, :nh]

    if prompt:
        i = pl.program_id(0)

        @pl.when(i % tiles_per_seq == 0)
        def _():
            carry_ref[...] = jnp.zeros_like(carry_ref)

        row = lax.broadcasted_iota(jnp.int32, (tm, tm), 0)
        col = lax.broadcasted_iota(jnp.int32, (tm, tm), 1)
        tril = jnp.where(col <= row, 1.0, 0.0).astype(BF16)
        pieces = jnp.concatenate(_split3(lf), axis=1)
        cs = jnp.dot(tril, pieces, preferred_element_type=F32)
        fc = cs[:, :LANES] + cs[:, LANES:2 * LANES] + cs[:, 2 * LANES:] + carry_ref[...]
        carry_ref[...] = fc[tm - 1:tm, :]

        fpieces = jnp.concatenate(_split3(fc * LOG2E), axis=1)
        kaug = jnp.dot(fpieces, selk_ref[...], preferred_element_type=F32) + onek_ref[...]
        qaug = jnp.dot(fpieces, selq_ref[...], preferred_element_type=F32) + oneq_ref[...]
        for p in range(heads // 2):
            ka_ref[p, :, LANES:] = kaug[:, p * LANES:(p + 1) * LANES].astype(BF16)
        for hd in range(heads):
            qa_ref[hd, :, LANES:] = qaug[:, hd * LANES:(hd + 1) * LANES].astype(BF16)

    q = proj("q") * (HEAD_DIM ** -0.5)
    k = proj("k")
    v = proj("v")
    for hd in range(heads):
        hs = slice(hd * HEAD_DIM, (hd + 1) * HEAD_DIM)
        k32_ref[0, pl.ds(hd, tm, stride=heads), :] = k[:, hs]
        v32_ref[0, pl.ds(hd, tm, stride=heads), :] = v[:, hs]
    if prompt:
        sq = jnp.concatenate([q * q, k * k], axis=1) * (1.0 + 2.0 ** -7)
        n2 = jnp.max(jnp.dot(sq.astype(BF16), hsel_ref[...], preferred_element_type=F32),
                     axis=0, keepdims=True)
        st_ref[0] = jnp.concatenate([n2[:, :LANES], n2[:, LANES:], fc[0:1, :], fc[tm - 1:tm, :],
                                     jnp.zeros((4, LANES), F32)], axis=0)

        lane_head = lax.broadcasted_iota(jnp.int32, (1, LANES), 1) // HEAD_DIM
        q2 = q * LOG2E
        for p in range(heads // 2):
            sl = slice(p * LANES, (p + 1) * LANES)
            ka_ref[p, :, :LANES] = k[:, sl].astype(BF16)
            for hh in range(2):
                hd = 2 * p + hh
                own = lane_head == hh
                qa_ref[hd, :, :LANES] = jnp.where(own, q2[:, sl], 0.0).astype(BF16)
                vt_ref[hd, 0] = jnp.where(own, v[:, sl], 1.0).T.astype(BF16)
    else:
        q_ref[...] = q.astype(BF16)
        kb_ref[...] = k.astype(BF16)
        vb_ref[...] = v.astype(BF16)

    u = jax.nn.gelu(proj("u"))
    sv = jax.nn.gelu(proj("sv"))
    mu = jnp.mean(sv, axis=-1, keepdims=True)
    var = jnp.mean(jnp.square(sv - mu), axis=-1, keepdims=True)
    sv = (sv - mu) * lax.rsqrt(var + GN_EPS) * lng_ref[...] + lnb_ref[...]
    if prompt:
        groups, chunk = sgw_ref.shape[0], sgw_ref.shape[1]
        crow = lax.broadcasted_iota(jnp.int32, (chunk, chunk), 0)
        ccol = lax.broadcasted_iota(jnp.int32, (chunk, chunk), 1)
        grp = lax.broadcasted_iota(jnp.int32, (1, sv.shape[1]), 1) // HEAD_DIM
        wm = [jnp.where(ccol <= crow, sgw_ref[g], 0.0).astype(BF16) for g in range(groups)]
        svb = sv.astype(BF16)
        for c in range(tm // chunk):
            rows = slice(c * chunk, (c + 1) * chunk)
            mixed = jnp.zeros((chunk, sv.shape[1]), F32)
            for g in range(groups):
                mixed = jnp.where(grp == g, jnp.dot(wm[g], svb[rows], preferred_element_type=F32), mixed)
            ob_ref[rows, :] = (u[rows] * (mixed + sgb_ref[...])).astype(BF16)
    else:
        u_ref[...] = u
        sv_ref[...] = sv

    cos = cos_ref[...]
    sin = sin_ref[...]
    rq_ref[...] = (proj("rq") * cos + proj("rqr") * sin).astype(BF16)
    rk_ref[...] = ((proj("rk") * cos + proj("rkr") * sin) * (HEAD_DIM ** -0.5)).astype(BF16)
    rv_ref[...] = proj("rv").astype(BF16)
    rg_ref[...] = _silu(proj("rg")).astype(BF16)


def _bias_selectors(heads):
    npairs = heads // 2
    selk = np.zeros((3 * LANES, npairs * LANES), np.float32)
    onek = np.zeros((1, npairs * LANES), np.float32)
    selq = np.zeros((3 * LANES, heads * LANES), np.float32)
    oneq = np.zeros((1, heads * LANES), np.float32)
    for p in range(npairs):
        for a in range(3):
            for hh in range(2):
                selk[a * LANES + 2 * p + hh, p * LANES + 3 * hh + a] = -1.0
            onek[0, p * LANES + 6 + a] = 1.0
    for hd in range(heads):
        for a in range(3):
            selq[a * LANES + hd, hd * LANES + 6 + a] = 1.0
            oneq[0, hd * LANES + 3 * (hd % 2) + a] = 1.0
    faw = heads * HEAD_DIM
    col = np.concatenate([np.arange(faw) // HEAD_DIM, LANES + np.arange(faw) // HEAD_DIM])
    hsel = col[:, None] == np.arange(2 * LANES)[None, :]
    return (jnp.asarray(selk, BF16), jnp.asarray(onek), jnp.asarray(selq, BF16), jnp.asarray(oneq),
            jnp.asarray(hsel, BF16))


def _inproj(h, g, w, bf, cos, sin, lng, lnb, kbuf, vbuf, *, layer, offs, dims, seq_len, prompt, sgu=None):
    n, d = h.shape
    tm = TOKEN_TILE if n % TOKEN_TILE == 0 else n
    nt = n // tm
    tiles_per_seq = max(seq_len // tm, 1)
    tab_blocks = cos.shape[0] // tm
    faw, sw, rw, nh = dims["faw"], dims["sw"], dims["rw"], dims["heads"]
    npairs = nh // 2
    tile = lambda width: pl.BlockSpec((tm, width), lambda i: (i, 0))
    tab = pl.BlockSpec((tm, rw), lambda i: (i % tab_blocks, 0))
    hbm = pl.BlockSpec(memory_space=pl.ANY)
    in_specs = [tile(d), _full((1, d)), _resident(w.shape), _full((1, LANES)), tab, tab,
                _full((1, sw)), _full((1, sw)), hbm, hbm]
    args = [h, g, w, bf, cos, sin, lng, lnb, kbuf, vbuf]
    layer_rows = pl.BlockSpec((1, tm * nh, HEAD_DIM), lambda i: (layer, i, 0))
    out_shape = [jax.ShapeDtypeStruct(kbuf.shape, F32), jax.ShapeDtypeStruct(vbuf.shape, F32)]
    out_specs = [layer_rows, layer_rows]
    bf_rows = jax.ShapeDtypeStruct((n, faw), BF16)
    if prompt:
        assert tm == ATTN_TILE
        sels = _bias_selectors(nh) + tuple(sgu)
        assert tm % sgu[0].shape[1] == 0
        in_specs += [_full(s.shape) for s in sels]
        args += list(sels)
        out_shape += [jax.ShapeDtypeStruct((nh, n, 2 * LANES), BF16),
                      jax.ShapeDtypeStruct((npairs, n, 2 * LANES), BF16),
                      jax.ShapeDtypeStruct((nh, nt, LANES, tm), BF16),
                      jax.ShapeDtypeStruct((nt, 8, LANES), F32)]
        out_specs += [pl.BlockSpec((nh, tm, 2 * LANES), lambda i: (0, i, 0)),
                      pl.BlockSpec((npairs, tm, 2 * LANES), lambda i: (0, i, 0)),
                      pl.BlockSpec((nh, 1, LANES, tm), lambda i: (0, i, 0, 0)),
                      pl.BlockSpec((1, 8, LANES), lambda i: (i, 0, 0))]
    else:
        out_shape += [bf_rows, bf_rows, bf_rows]
        out_specs += [tile(faw)] * 3
    out_shape.append(jax.ShapeDtypeStruct((n, nh), F32))
    out_specs.append(tile(nh))
    if prompt:
        out_shape.append(jax.ShapeDtypeStruct((n, sw), BF16))
        out_specs.append(tile(sw))
    else:
        out_shape += [jax.ShapeDtypeStruct((n, sw), F32), jax.ShapeDtypeStruct((n, sw), F32)]
        out_specs += [tile(sw), tile(sw)]
    out_shape += [jax.ShapeDtypeStruct((n, rw), BF16)] * 4
    out_specs += [tile(rw)] * 4
    scratch = [pltpu.VMEM((1, LANES), F32)] if prompt else []
    return pl.pallas_call(
        functools.partial(_inproj_kernel, offs=offs, tiles_per_seq=tiles_per_seq,
                          prompt=prompt, tm=tm, heads=nh),
        grid=(nt,), in_specs=in_specs, out_specs=out_specs, out_shape=out_shape,
        scratch_shapes=scratch, input_output_aliases={8: 0, 9: 1},
        compiler_params=_cparams(("arbitrary",)),
        name="inproj",
    )(*args)


def _fox_prompt_kernel(first_ref, fast_ref, q_ref, k_ref, vt_ref, o_ref, m_ref, acc_ref, *, t):
    qi = pl.program_id(2)
    idx = (pl.program_id(0) * pl.num_programs(1) + pl.program_id(1)) * pl.num_programs(2) + qi
    first = first_ref[idx]
    acc_ref[...] = jnp.zeros(acc_ref.shape, F32)

    def scores(ki, hh, masked):
        start = pl.multiple_of(ki * t, t)
        s = _nt_dot(k_ref[0, pl.ds(start, t), :], q_ref[hh])
        if masked:
            key = lax.broadcasted_iota(jnp.int32, (t, t), 0)
            qry = lax.broadcasted_iota(jnp.int32, (t, t), 1)
            s = jnp.where(key <= qry, s, NEG_INF)
        return s

    def plain_step(ki, masked):
        pr = [jnp.exp2(scores(ki, hh, masked)).astype(BF16) for hh in range(2)]
        for hh in range(2):
            acc_ref[hh] += jnp.dot(vt_ref[hh, ki], pr[hh], preferred_element_type=F32)

    def online_step(ki, masked):
        for hh in range(2):
            s = scores(ki, hh, masked)
            m_prev = m_ref[hh]
            m_new = jnp.maximum(m_prev, jnp.max(s, axis=0, keepdims=True))
            pr = jnp.exp2(s - m_new).astype(BF16)
            acc_ref[hh] = (jnp.exp2(m_prev - m_new) * acc_ref[hh]
                           + jnp.dot(vt_ref[hh, ki], pr, preferred_element_type=F32))
            m_ref[hh] = m_new

    def run(step):
        def body(ki, carry):
            step(ki, False)
            return carry

        lax.fori_loop(first, qi, body, 0)
        step(qi, True)

    def plain_pair(ki):
        start = pl.multiple_of(ki * t, t)
        ks = k_ref[0, pl.ds(start, 2 * t), :]
        pr = [jnp.exp2(_nt_dot(ks, q_ref[hh])).astype(BF16) for hh in range(2)]
        for hh in range(2):
            vts = jnp.concatenate([vt_ref[hh, ki], vt_ref[hh, ki + 1]], axis=1)
            acc_ref[hh] += jnp.dot(vts, pr[hh], preferred_element_type=F32)

    def run_plain():
        past = qi - first

        def body(j, carry):
            plain_pair(first + 2 * j)
            return carry

        lax.fori_loop(0, past // 2, body, 0)

        @pl.when(past % 2 == 1)
        def _():
            plain_step(jnp.maximum(qi - 1, 0), False)

        plain_step(qi, True)

    bounded = fast_ref[idx] == 1

    @pl.when(bounded)
    def _():
        run_plain()

    @pl.when(jnp.logical_not(bounded))
    def _():
        m_ref[...] = jnp.full(m_ref.shape, NEG_INF, F32)
        run(online_step)

    acc_a = acc_ref[0]
    acc_b = acc_ref[1]
    head_a = lax.broadcasted_iota(jnp.int32, (LANES, 1), 0) < HEAD_DIM
    o_t = jnp.where(head_a, acc_a / acc_a[HEAD_DIM:HEAD_DIM + 1, :], acc_b / acc_b[0:1, :])
    o_ref[...] = o_t.T.astype(BF16)


F32_EXP_ZERO = -104.0
PLAIN_SCORE_LIMIT = 30.0


def _attention_plan(stats, *, batch, nq, heads):
    st = stats.reshape(batch, nq, 8, LANES)[..., :heads]
    grow = 1.0 + 2.0 ** -6
    qn = jnp.sqrt(st[:, :, 0]) * grow
    kn = jnp.sqrt(jnp.max(st[:, :, 1], axis=1, keepdims=True)) * grow
    u = qn * kn
    plain = jnp.all((u <= PLAIN_SCORE_LIMIT).reshape(batch, nq, heads // 2, 2), axis=-1)
    plain_h = jnp.repeat(plain, 2, axis=-1)
    reach = jnp.where(plain_h, u, 2.0 * u)
    f_first, f_last = st[:, :, 2], st[:, :, 3]
    bound = reach[:, :, None, :] + f_first[:, :, None, :] - f_last[:, None, :, :] + 1.0
    past = (jnp.arange(nq)[None, :] < jnp.arange(nq)[:, None])[None, :, :, None]
    dead = jnp.argmax(~(past & (bound < F32_EXP_ZERO)), axis=2).astype(jnp.int32)
    dead = jnp.min(dead.reshape(batch, nq, heads // 2, 2), axis=-1)
    flat = lambda a: a.transpose(0, 2, 1).reshape(-1).astype(jnp.int32)
    return flat(dead), flat(plain)


def _fox_prompt(qa, ka, vt, stats, *, batch, seq, heads):
    n = qa.shape[1]
    t = ATTN_TILE
    nq = seq // t
    npairs = heads // 2
    first, plain = _attention_plan(stats, batch=batch, nq=nq, heads=heads)
    grid_spec = pltpu.PrefetchScalarGridSpec(
        num_scalar_prefetch=2,
        grid=(batch, npairs, nq),
        in_specs=[pl.BlockSpec((2, t, 2 * LANES), lambda b, p, i, *_: (p, b * nq + i, 0)),
                  pl.BlockSpec((1, seq, 2 * LANES), lambda b, p, i, *_: (p, b, 0)),
                  pl.BlockSpec((2, nq, LANES, t), lambda b, p, i, *_: (p, b, 0, 0))],
        out_specs=pl.BlockSpec((t, LANES), lambda b, p, i, *_: (b * nq + i, p)),
        scratch_shapes=[pltpu.VMEM((2, 1, t), F32), pltpu.VMEM((2, LANES, t), F32)],
    )
    return pl.pallas_call(
        functools.partial(_fox_prompt_kernel, t=t),
        grid_spec=grid_spec,
        out_shape=jax.ShapeDtypeStruct((n, heads * HEAD_DIM), BF16),
        compiler_params=_cparams(("arbitrary", "arbitrary", "arbitrary")),
        name="fox_prompt",
    )(first, plain, qa, ka, vt)


def _fox_sample_kernel(q_ref, kn_ref, vn_ref, pk_ref, pv_ref, plft_ref, lf_ref, lft_ref, o_ref,
                       *, heads, ts, past):
    hi = lax.Precision.HIGHEST
    r_ = lax.broadcasted_iota(jnp.int32, (past, past), 0)
    c_ = lax.broadcasted_iota(jnp.int32, (past, past), 1)
    after = jnp.where(r_ > c_, 1.0, 0.0).astype(F32)
    g_row = jnp.dot(plft_ref[0], after, precision=hi, preferred_element_type=F32)
    rr = lax.broadcasted_iota(jnp.int32, (LANES, LANES), 0)
    cc = lax.broadcasted_iota(jnp.int32, (LANES, LANES), 1)
    tril = jnp.where(cc <= rr, 1.0, 0.0).astype(F32)
    triu = jnp.where(rr <= cc, 1.0, 0.0).astype(F32)
    c_col = jnp.dot(tril, lf_ref[0], precision=hi, preferred_element_type=F32)[:ts]
    c_row = jnp.dot(lft_ref[0], triu, precision=hi, preferred_element_type=F32)
    visible = (lax.broadcasted_iota(jnp.int32, (ts, LANES), 1)
               <= lax.broadcasted_iota(jnp.int32, (ts, LANES), 0))
    lane = lax.broadcasted_iota(jnp.int32, (1, LANES), 1)
    lo = lane < HEAD_DIM
    for p in range(heads // 2):
        sl = slice(p * LANES, (p + 1) * LANES)
        q2 = q_ref[:, sl]
        kp2 = pk_ref[0, :, sl].astype(BF16)
        vp2 = pv_ref[0, :, sl].astype(BF16)
        kn2 = kn_ref[:, sl]
        vn2 = vn_ref[:, sl]
        zero = jnp.zeros_like(q2)
        res = []
        for hh in range(2):
            h = 2 * p + hh
            qh = jnp.where(lo, q2, zero) if hh == 0 else jnp.where(lo, zero, q2)
            s_p = _nt_dot(qh, kp2) + c_col[:, h:h + 1] + g_row[h:h + 1, :]
            s_n = _nt_dot(qh, kn2) + c_col[:, h:h + 1] - c_row[h:h + 1, :]
            s_n = jnp.where(visible, s_n, NEG_INF)
            m = jnp.maximum(jnp.max(s_p, axis=1, keepdims=True), jnp.max(s_n, axis=1, keepdims=True))
            pp = jnp.exp(s_p - m)
            pn = jnp.exp(s_n - m)
            l = jnp.sum(pp, axis=1, keepdims=True) + jnp.sum(pn, axis=1, keepdims=True)
            o = (jnp.dot(pp.astype(BF16), vp2, preferred_element_type=F32)
                 + jnp.dot(pn.astype(BF16), vn2, preferred_element_type=F32))
            res.append(o / l)
        o_ref[:, sl] = jnp.where(lo, res[0], res[1]).astype(BF16)


def _fox_sample(qb, kb, vb, lf, past_k, past_v, past_lf, *, batch, ts, heads):
    n, faw = qb.shape
    past = past_k.shape[1]
    plft = past_lf.transpose(0, 2, 1)
    lf3 = lf.reshape(batch, ts, heads)
    lfp = jnp.pad(lf3, ((0, 0), (0, LANES - ts), (0, LANES - heads)))
    lftp = jnp.pad(lf3.transpose(0, 2, 1), ((0, 0), (0, 0), (0, LANES - ts)))
    padr = lambda a: jnp.pad(a.reshape(batch, ts, faw), ((0, 0), (0, LANES - ts), (0, 0))
                             ).reshape(batch * LANES, faw)
    row = lambda rows: pl.BlockSpec((rows, faw), lambda b: (b, 0))
    return pl.pallas_call(
        functools.partial(_fox_sample_kernel, heads=heads, ts=ts, past=past),
        grid=(batch,),
        in_specs=[row(ts), row(LANES), row(LANES),
                  pl.BlockSpec((1, past, faw), lambda b: (b, 0, 0)),
                  pl.BlockSpec((1, past, faw), lambda b: (b, 0, 0)),
                  pl.BlockSpec((1, heads, past), lambda b: (b, 0, 0)),
                  pl.BlockSpec((1, LANES, LANES), lambda b: (b, 0, 0)),
                  pl.BlockSpec((1, heads, LANES), lambda b: (b, 0, 0))],
        out_specs=row(ts),
        out_shape=jax.ShapeDtypeStruct((n, faw), BF16),
        compiler_params=_cparams(("arbitrary",)),
        name="fox_sample",
    )(qb, padr(kb), padr(vb), past_k, past_v, plft, lfp, lftp)


def _sgu_kernel(u_ref, v_ref, w_ref, b_ref, o_ref, *, chunk, chunks_per_tile, groups):
    width = u_ref.shape[-1]
    row = lax.broadcasted_iota(jnp.int32, (chunk, chunk), 0)
    col = lax.broadcasted_iota(jnp.int32, (chunk, chunk), 1)
    causal = col <= row
    grp = lax.broadcasted_iota(jnp.int32, (1, width), 1) // HEAD_DIM
    wm = [jnp.where(causal, w_ref[g], 0.0).astype(BF16) for g in range(groups)]
    bias = b_ref[...]
    for c in range(chunks_per_tile):
        sl = slice(c * chunk, (c + 1) * chunk)
        vc = v_ref[sl, :].astype(BF16)
        mixed = jnp.zeros((chunk, width), F32)
        for g in range(groups):
            mixed = jnp.where(grp == g, jnp.dot(wm[g], vc, preferred_element_type=F32), mixed)
        o_ref[sl, :] = (u_ref[sl, :] * (mixed + bias)).astype(BF16)


def _sgu(u, v, w, bias):
    n, width = u.shape
    groups, chunk, _ = w.shape
    tile = TOKEN_TILE if (n % TOKEN_TILE == 0 and TOKEN_TILE % chunk == 0) else chunk
    tok = pl.BlockSpec((tile, width), lambda i: (i, 0))
    return pl.pallas_call(
        functools.partial(_sgu_kernel, chunk=chunk, chunks_per_tile=tile // chunk, groups=groups),
        grid=(n // tile,),
        in_specs=[tok, tok, _full(w.shape), _full(bias.shape)],
        out_specs=tok,
        out_shape=jax.ShapeDtypeStruct((n, width), BF16),
        compiler_params=_cparams(("arbitrary",)),
        name="sgu",
    )(u, v, w, bias)


def _retention_kernel(q_ref, k_ref, v_ref, g_ref, s0_ref, dmat_ref, qdec_ref, kdec_ref, sdec_ref,
                      bmask_ref, avg_ref, gn_ref, o_ref, sfin_ref, s_ref, *, npairs):
    c = pl.program_id(1)

    @pl.when(c == 0)
    def _():
        s_ref[...] = s0_ref[0]

    lane = lax.broadcasted_iota(jnp.int32, (1, LANES), 1)
    lo = lane < HEAD_DIM
    avg = avg_ref[...]

    def group_mean(x):
        x1 = x.astype(BF16)
        x2 = (x - x1.astype(F32)).astype(BF16)
        return (jnp.dot(x1, avg, preferred_element_type=F32)
                + jnp.dot(x2, avg, preferred_element_type=F32))

    for p in range(npairs):
        sl = slice(p * LANES, (p + 1) * LANES)
        q2 = q_ref[:, sl]
        k2 = k_ref[:, sl]
        v2 = v_ref[:, sl]
        zero = jnp.zeros_like(q2)
        inner = []
        for hh in range(2):
            qh = jnp.where(lo, q2, zero) if hh == 0 else jnp.where(lo, zero, q2)
            sc = _nt_dot(qh, k2) * dmat_ref[2 * p + hh]
            inner.append(jnp.dot(sc.astype(BF16), v2, preferred_element_type=F32))
        state = s_ref[p]
        cross = jnp.dot(q2, state.astype(BF16), preferred_element_type=F32) * qdec_ref[:, sl]
        o = jnp.where(lo, inner[0], inner[1]) + cross
        kd_t = (k2.astype(F32) * kdec_ref[:, sl]).T.astype(BF16)
        s_ref[p] = state * sdec_ref[p] + jnp.dot(kd_t, v2, preferred_element_type=F32) * bmask_ref[...]
        mu = group_mean(o)
        d = o - mu
        var = group_mean(d * d)
        on = d * lax.rsqrt(var + GN_EPS)
        o_ref[:, sl] = (on * gn_ref[:, sl] * g_ref[:, sl].astype(F32)).astype(BF16)

    sfin_ref[0] = s_ref[...]


def _retention(rq, rk, rv, rg, s0, tabs, gn, *, batch, rows_per_seq, blk):
    n, rw = rq.shape
    npairs = rw // LANES
    nc = rows_per_seq // blk
    tok = pl.BlockSpec((blk, rw), lambda b, c: (b * nc + c, 0))
    st = pl.BlockSpec((1, npairs, LANES, LANES), lambda b, c: (b, 0, 0, 0))
    dmat, qdec, kdec, sdec, bmask, avg = tabs
    return pl.pallas_call(
        functools.partial(_retention_kernel, npairs=npairs),
        grid=(batch, nc),
        in_specs=[tok, tok, tok, tok, st, _full(dmat.shape), _full(qdec.shape), _full(kdec.shape),
                  _full(sdec.shape), _full(bmask.shape), _full(avg.shape), _full(gn.shape)],
        out_specs=[tok, st],
        out_shape=[jax.ShapeDtypeStruct((n, rw), BF16),
                   jax.ShapeDtypeStruct((batch, npairs, LANES, LANES), F32)],
        scratch_shapes=[pltpu.VMEM((npairs, LANES, LANES), F32)],
        compiler_params=_cparams(("arbitrary", "arbitrary")),
        name="retention",
    )(rq, rk, rv, rg, s0, dmat, qdec, kdec, sdec, bmask, avg, gn)


def _retention_tables(ret_heads, true_len, blk):
    log_g = jnp.log1p(-jnp.exp(jnp.linspace(math.log(RET_DECAY_MIN), math.log(RET_DECAY_MAX),
                                            ret_heads, dtype=F32)))
    n = jnp.arange(blk, dtype=F32)
    diff = n[:, None] - n[None, :]
    causal = diff >= 0
    dmat = jnp.where(causal[None], jnp.exp(jnp.where(causal, diff, 0.0)[None] * log_g[:, None, None]), 0.0)
    per_lane = jnp.repeat(log_g, HEAD_DIM)
    qdec = jnp.exp((n[:, None] + 1.0) * per_lane[None, :])
    kdec = jnp.exp((true_len - 1.0 - n)[:, None] * per_lane[None, :])
    npairs = ret_heads // 2
    lane_head = jnp.arange(LANES) // HEAD_DIM
    bmask = (lane_head[:, None] == lane_head[None, :]).astype(F32)
    sdec = jnp.exp(true_len * per_lane).reshape(npairs, LANES)[:, :, None] * bmask[None]
    avg = (bmask / HEAD_DIM).astype(BF16)
    return dmat, qdec, kdec, sdec, bmask, avg


def _outproj_kernel(h_ref, oa_ref, ob_ref, oc_ref, w_ref, g_ref, wr_ref, br_ref,
                    hnew_ref, hn_ref, eg_ref, *, n_experts, faw, sw):
    o = (jnp.dot(oa_ref[...], w_ref[:faw, :], preferred_element_type=F32)
         + jnp.dot(ob_ref[...], w_ref[faw:faw + sw, :], preferred_element_type=F32)
         + jnp.dot(oc_ref[...], w_ref[faw + sw:, :], preferred_element_type=F32))
    h = h_ref[...] + o
    hnew_ref[...] = h
    hn = _rms(h, g_ref[...])
    hn_ref[...] = hn.astype(BF16)
    x1 = hn.astype(BF16)
    x2 = (hn - x1.astype(F32)).astype(BF16)
    a = jnp.dot(x1, wr_ref[...], preferred_element_type=F32)
    b = jnp.dot(x2, wr_ref[:, :LANES], preferred_element_type=F32)
    logits = a[:, :LANES] + a[:, LANES:] + b + br_ref[...]
    lane = lax.broadcasted_iota(jnp.int32, logits.shape, 1)
    lanef = lane.astype(F32)
    logits = jnp.where(lane < n_experts, logits, -jnp.inf)
    m1 = jnp.max(logits, axis=1, keepdims=True)
    i1 = jnp.min(jnp.where(logits == m1, lanef, float(LANES)), axis=1, keepdims=True)
    rest = jnp.where(lanef == i1, -jnp.inf, logits)
    m2 = jnp.max(rest, axis=1, keepdims=True)
    i2 = jnp.min(jnp.where(rest == m2, lanef, float(LANES)), axis=1, keepdims=True)
    e2 = jnp.exp(m2 - m1)
    g1 = 1.0 / (1.0 + e2)
    g2 = e2 / (1.0 + e2)
    eg_ref[...] = jnp.where(lane == 0, g1, jnp.where(lane == 1, g2,
                            jnp.where(lane == 2, i1, jnp.where(lane == 3, i2, 0.0))))


def _outproj(h, oa, ob, oc, w, g, router):
    n, d = h.shape
    tm = TOKEN_TILE if n % TOKEN_TILE == 0 else n
    faw, sw, rw = oa.shape[1], ob.shape[1], oc.shape[1]
    wr, br, n_experts = router
    tile = lambda width: pl.BlockSpec((tm, width), lambda i: (i, 0))
    return pl.pallas_call(
        functools.partial(_outproj_kernel, n_experts=n_experts, faw=faw, sw=sw),
        grid=(n // tm,),
        in_specs=[tile(d), tile(faw), tile(sw), tile(rw), _resident(w.shape), _full((1, d)),
                  _full(wr.shape), _full(br.shape)],
        out_specs=[tile(d), tile(d), tile(LANES)],
        out_shape=[jax.ShapeDtypeStruct((n, d), F32), jax.ShapeDtypeStruct((n, d), BF16),
                   jax.ShapeDtypeStruct((n, LANES), F32)],
        compiler_params=_cparams(("arbitrary",)),
        name="outproj",
    )(h, oa, ob, oc, w, g, wr, br)


def _dense_tail_kernel(*refs, final, fc, nchunks, faw, sw):
    (h_ref, oa_ref, ob_ref, oc_ref, wo_ref, gffn_ref, wg_ref, wu_ref, wd_ref,
     p_ref, gple_ref, wpg_ref, wpp_ref) = refs[:13]
    rest = list(refs[13:])
    gf_ref = rest.pop(0) if final else None
    o_ref, acc_ref = rest
    o = (jnp.dot(oa_ref[...], wo_ref[:faw, :], preferred_element_type=F32)
         + jnp.dot(ob_ref[...], wo_ref[faw:faw + sw, :], preferred_element_type=F32)
         + jnp.dot(oc_ref[...], wo_ref[faw + sw:, :], preferred_element_type=F32))
    h = h_ref[...] + o
    x = _rms(h, gffn_ref[...]).astype(BF16)
    for c in range(nchunks):
        sl = slice(c * fc, (c + 1) * fc)
        g = jnp.dot(x, wg_ref[:, sl], preferred_element_type=F32)
        u = jnp.dot(x, wu_ref[:, sl], preferred_element_type=F32)
        a = (_silu(g) * u).astype(BF16)
        y = jnp.dot(a, wd_ref[sl, :], preferred_element_type=F32)
        if c == 0:
            acc_ref[...] = y
        else:
            acc_ref[...] += y
    h = h + acc_ref[...]
    gate = jax.nn.sigmoid(jnp.dot(_rms(h, gple_ref[...]).astype(BF16), wpg_ref[...],
                                  preferred_element_type=F32))
    h = h + gate * jnp.dot(p_ref[0].astype(BF16), wpp_ref[...], preferred_element_type=F32)
    if final:
        h = _rms(h, gf_ref[...])
    o_ref[...] = h


def _dense_tail(h, oa, ob, oc, wo, gffn, wg, wu, wd, p_all, gple, wpg, wpp, g_final=None, *, layer):
    n, d = h.shape
    ff = wg.shape[1]
    tm = TOKEN_TILE if n % TOKEN_TILE == 0 else n
    fc = MXU_DEPTH if ff % MXU_DEPTH == 0 else ff
    faw, sw, rw = oa.shape[1], ob.shape[1], oc.shape[1]
    tile = lambda width: pl.BlockSpec((tm, width), lambda i: (i, 0))
    in_specs = [tile(d), tile(faw), tile(sw), tile(rw), _resident(wo.shape), _full((1, d)),
                _resident(wg.shape), _resident(wu.shape), _resident(wd.shape),
                pl.BlockSpec((1, tm, p_all.shape[2]), lambda i: (layer, i, 0)), _full((1, d)),
                _resident(wpg.shape), _resident(wpp.shape)]
    args = [h, oa, ob, oc, wo, gffn, wg, wu, wd, p_all, gple, wpg, wpp]
    if g_final is not None:
        in_specs.append(_full((1, d)))
        args.append(g_final)
    return pl.pallas_call(
        functools.partial(_dense_tail_kernel, final=g_final is not None, fc=fc, nchunks=ff // fc,
                          faw=faw, sw=sw),
        grid=(n // tm,), in_specs=in_specs, out_specs=tile(d),
        out_shape=jax.ShapeDtypeStruct((n, d), F32),
        scratch_shapes=[pltpu.VMEM((tm, d), F32)],
        compiler_params=_cparams(("arbitrary",)),
        name="dense_tail",
    )(*args)


def _moe_ffn_kernel(be_ref, nu_ref, x_ref, wg_ref, wu_ref, wd_ref, o_ref, acc_ref, *, nch, subs):
    b = pl.program_id(0)
    c = pl.program_id(1)

    @pl.when(b >= nu_ref[0])
    def _():
        o_ref[...] = jnp.zeros_like(o_ref)

    @pl.when(b < nu_ref[0])
    def _():
        @pl.when(c == 0)
        def _():
            acc_ref[...] = jnp.zeros_like(acc_ref)

        x = x_ref[...]
        for a0, a1 in subs:
            g = jnp.dot(x, wg_ref[0, :, a0:a1], preferred_element_type=F32)
            u = jnp.dot(x, wu_ref[0, :, a0:a1], preferred_element_type=F32)
            a = (_silu(g) * u).astype(BF16)
            acc_ref[...] += jnp.dot(a, wd_ref[0, a0:a1, :], preferred_element_type=F32)

        @pl.when(c == nch - 1)
        def _():
            o_ref[...] = acc_ref[...].astype(BF16)


def _moe_ffn(xs, block_expert, n_used, wg, wu, wd, *, rb):
    r, d = xs.shape
    ff = wg.shape[2]
    nblk = r // rb
    nch = 2 if ff % (2 * MXU_DEPTH) == 0 else 1
    fc = ff // nch
    step = 2 * MXU_DEPTH
    subs = tuple((a0, min(a0 + step, fc)) for a0 in range(0, fc, step))

    def live(b, nu):
        return jnp.maximum(jnp.minimum(b, nu[0] - 1), 0)

    def chunk(b, c, nu):
        return jnp.where(b < nu[0], c, nch - 1)

    grid_spec = pltpu.PrefetchScalarGridSpec(
        num_scalar_prefetch=2,
        grid=(nblk, nch),
        in_specs=[pl.BlockSpec((rb, d), lambda b, c, be, nu: (live(b, nu), 0)),
                  pl.BlockSpec((1, d, fc), lambda b, c, be, nu: (be[live(b, nu)], 0, chunk(b, c, nu))),
                  pl.BlockSpec((1, d, fc), lambda b, c, be, nu: (be[live(b, nu)], 0, chunk(b, c, nu))),
                  pl.BlockSpec((1, fc, d), lambda b, c, be, nu: (be[live(b, nu)], chunk(b, c, nu), 0))],
        out_specs=pl.BlockSpec((rb, d), lambda b, c, be, nu: (b, 0)),
        scratch_shapes=[pltpu.VMEM((rb, d), F32)],
    )
    return pl.pallas_call(
        functools.partial(_moe_ffn_kernel, nch=nch, subs=subs),
        grid_spec=grid_spec,
        out_shape=jax.ShapeDtypeStruct((r, d), BF16),
        compiler_params=_cparams(("arbitrary", "arbitrary"), MOE_VMEM_LIMIT),
        name="moe_ffn",
    )(block_expert, n_used, xs, wg, wu, wd)


def _gather_rows(x, idx):
    return x.at[idx].get(mode="promise_in_bounds")


def _moe(hn, eg, wg, wu, wd, *, rb, n_experts, moe_layer):
    n, d = hn.shape
    rows = TOP_K * n
    flat_e = eg[:, TOP_K:2 * TOP_K].astype(jnp.int32).T.reshape(-1)
    onehot = (jnp.arange(n_experts, dtype=jnp.int32)[:, None] == flat_e[None, :]).astype(jnp.int32)
    csum = jnp.cumsum(onehot, axis=1)
    count = csum[:, -1]
    first_sorted = jnp.cumsum(count) - count
    padded = ((count + rb - 1) // rb) * rb
    ends = jnp.cumsum(padded)
    offset = ends - padded
    pos = jnp.sum(onehot * (csum - 1 + offset[:, None]), axis=0)
    nblk = pl.cdiv(rows, rb) + n_experts
    r_tot = nblk * rb
    block_start = jnp.arange(nblk, dtype=jnp.int32) * rb
    block_expert = jnp.minimum(jnp.sum((block_start[:, None] >= ends[None, :]).astype(jnp.int32), axis=1),
                               n_experts - 1).astype(jnp.int32)
    n_used = (ends[-1:] // rb).astype(jnp.int32)
    order = jnp.argsort(flat_e, stable=True).astype(jnp.int32)
    slot_e = jnp.repeat(block_expert, rb)
    local = jnp.arange(r_tot, dtype=jnp.int32) - offset[slot_e]
    src_row = order[jnp.clip(first_sorted[slot_e] + local, 0, rows - 1)]
    xs = _gather_rows(hn, src_row % n)
    ys = _moe_ffn(xs, block_expert + moe_layer * n_experts, n_used, wg, wu, wd, rb=rb)
    return _gather_rows(ys, pos)


def _ple_kernel(*refs, final):
    (h_ref, y1_ref, y2_ref, eg_ref, p_ref, g_ref, wg_ref, wp_ref) = refs[:8]
    gf_ref = refs[8] if final else None
    o_ref = refs[-1]
    gates = eg_ref[...]
    h = (h_ref[...] + gates[:, 0:1] * y1_ref[...].astype(F32)
         + gates[:, 1:2] * y2_ref[...].astype(F32))
    gate = jax.nn.sigmoid(jnp.dot(_rms(h, g_ref[...]).astype(BF16), wg_ref[...],
                                  preferred_element_type=F32))
    h = h + gate * jnp.dot(p_ref[0].astype(BF16), wp_ref[...], preferred_element_type=F32)
    if final:
        h = _rms(h, gf_ref[...])
    o_ref[...] = h


def _ple(h, y, eg, p_all, g, wg, wp, g_final=None, *, layer):
    n, d = h.shape
    tm = TOKEN_TILE if n % TOKEN_TILE == 0 else n
    nt = n // tm
    tile = lambda width: pl.BlockSpec((tm, width), lambda i: (i, 0))
    in_specs = [tile(d), tile(d), pl.BlockSpec((tm, d), lambda i: (i + nt, 0)), tile(LANES),
                pl.BlockSpec((1, tm, p_all.shape[2]), lambda i: (layer, i, 0)),
                _full((1, d)), _resident(wg.shape), _resident(wp.shape)]
    args = [h, y, y, eg, p_all, g, wg, wp]
    if g_final is not None:
        in_specs.append(_full((1, d)))
        args.append(g_final)
    return pl.pallas_call(
        functools.partial(_ple_kernel, final=g_final is not None),
        grid=(nt,), in_specs=in_specs, out_specs=tile(d),
        out_shape=jax.ShapeDtypeStruct((n, d), F32),
        compiler_params=_cparams(("arbitrary",)),
        name="ple",
    )(*args)


def _prep_w_in(w_in, dims):
    faw, sw, rw, nh = dims["faw"], dims["sw"], dims["rw"], dims["heads"]
    sizes = (faw, faw, faw, nh, sw, sw, rw, rw, rw, rw)
    parts, o = [], 0
    for s in sizes:
        parts.append(w_in[..., o:o + s])
        o += s
    fa_q, fa_k, fa_v, fa_f, sgu_u, sgu_v, ret_q, ret_k, ret_v, ret_g = parts

    def rot_cols(w):
        lead = w.shape[:-1]
        wh = w.reshape(lead + (rw // HEAD_DIM, 2, HEAD_DIM // 2))
        return jnp.stack([-wh[..., 1, :], wh[..., 0, :]], axis=-2).reshape(lead + (rw,))

    f_pad = jnp.pad(fa_f, [(0, 0)] * (w_in.ndim - 1) + [(0, LANES - nh)])
    cols = [("q", fa_q), ("k", fa_k), ("v", fa_v), ("u", sgu_u), ("sv", sgu_v), ("rq", ret_q),
            ("rqr", rot_cols(ret_q)), ("rk", ret_k), ("rkr", rot_cols(ret_k)), ("rv", ret_v),
            ("rg", ret_g), ("f", f_pad)]
    offs, o = {}, 0
    for name, c in cols:
        offs[name] = (o, o + c.shape[-1])
        o += c.shape[-1]
    return jnp.concatenate([c for _, c in cols], axis=-1).astype(BF16), offs


def _rope_tables(pos, ret_heads):
    half = HEAD_DIM // 2
    inv = ROPE_BASE ** (-jnp.arange(half, dtype=F32) / half)
    ang = pos.astype(F32)[:, None] * inv[None, :]
    cos = jnp.tile(jnp.cos(ang), (1, 2 * ret_heads))
    sin = jnp.tile(jnp.sin(ang), (1, 2 * ret_heads))
    return cos, sin


def _pair_states(s):
    b, hds = s.shape[:2]
    s = s.reshape(b, hds // 2, 2, HEAD_DIM, HEAD_DIM)
    z = jnp.zeros_like(s[:, :, 0])
    top = jnp.concatenate([s[:, :, 0], z], axis=-1)
    bot = jnp.concatenate([z, s[:, :, 1]], axis=-1)
    return jnp.concatenate([top, bot], axis=-2)


def _unpair_states(sp):
    a = sp[:, :, :HEAD_DIM, :HEAD_DIM]
    b = sp[:, :, HEAD_DIM:, HEAD_DIM:]
    return jnp.stack([a, b], axis=2).reshape(sp.shape[0], -1, HEAD_DIM, HEAD_DIM)


def _trunk(x, p, cache, wts, dims):
    batch, t, d = x.shape
    n = batch * t
    depth = wts["w_in"].shape[0]
    heads, ret_heads = dims["heads"], dims["ret_heads"]
    has_past = cache is not None
    offset = cache[0].shape[2] if has_past else 0
    pos = offset + jnp.arange(t, dtype=jnp.int32)
    cos, sin = _rope_tables(pos, ret_heads)
    if t < TOKEN_TILE:
        cos, sin = jnp.tile(cos, (batch, 1)), jnp.tile(sin, (batch, 1))

    if has_past:
        ret_blk, ret_rows = LANES, LANES
        tabs = _retention_tables(ret_heads, t, ret_blk)
    else:
        ret_blk = RET_CHUNK if t % RET_CHUNK == 0 else t
        ret_rows = t
        tabs = _retention_tables(ret_heads, ret_blk, ret_blk)

    sgu_full = wts["sgu_w"].shape[2]
    sgu_len = sgu_full if t >= sgu_full else t

    h = x.reshape(n, d)
    p_all = p.reshape(depth, n, -1)
    new_lf, new_ret, new_sgu = [], [], []
    kbuf = jnp.zeros((depth, n * heads, HEAD_DIM), F32)
    vbuf = jnp.zeros((depth, n * heads, HEAD_DIM), F32)
    for i in range(depth):
        sgu_w, sgu_b = wts["sgu_w"][i], wts["sgu_b"][i]
        groups = sgu_w.shape[0]
        w_l = sgu_w[:, :sgu_len, :sgu_len]
        b_l = jnp.repeat(sgu_b[:, :sgu_len].T, HEAD_DIM, axis=1)
        outs = _inproj(h, wts["g_mix"][i], wts["w_in"][i], wts["b_forget"][i], cos, sin,
                       wts["sgu_ln_g"][i], wts["sgu_ln_b"][i], kbuf, vbuf, layer=i,
                       offs=wts["offs"], dims=dims, seq_len=t, prompt=not has_past,
                       sgu=None if has_past else (w_l, b_l))
        if has_past:
            kbuf, vbuf, qb, kb, vb, lf, u, sv, rq, rk, rv, rg = outs
            oa = _fox_sample(qb, kb, vb, lf, cache[0][i].reshape(batch, offset, -1),
                             cache[1][i].reshape(batch, offset, -1), cache[2][i],
                             batch=batch, ts=t, heads=heads)
        else:
            kbuf, vbuf, qa, ka, vt, stats, lf, ob, rq, rk, rv, rg = outs
            oa = _fox_prompt(qa, ka, vt, stats, batch=batch, seq=t, heads=heads)

        if has_past:
            reps = n // sgu_len
            eye = jnp.eye(reps, dtype=F32)
            w_l = jnp.einsum("ab,gts->gatbs", eye, w_l).reshape(groups, n, n)
            ob = _sgu(u, sv, w_l, jnp.tile(b_l, (reps, 1)))
            new_sgu.append(sv.reshape(batch, t, -1))

        if has_past:
            padr = lambda a: jnp.pad(a.reshape(batch, t, -1), ((0, 0), (0, ret_rows - t), (0, 0))
                                     ).reshape(batch * ret_rows, -1)
            s0 = _pair_states(cache[3][i].astype(F32))
            oc, sfin = _retention(padr(rq), padr(rk), padr(rv), padr(rg), s0, tabs, wts["ret_gn_g"][i],
                                  batch=batch, rows_per_seq=ret_rows, blk=ret_blk)
            oc = oc.reshape(batch, ret_rows, -1)[:, :t].reshape(n, -1)
        else:
            s0 = jnp.zeros((batch, ret_heads // 2, LANES, LANES), F32)
            oc, sfin = _retention(rq, rk, rv, rg, s0, tabs, wts["ret_gn_g"][i],
                                  batch=batch, rows_per_seq=ret_rows, blk=ret_blk)

        j = i // 2
        is_moe = i % 2 == 1
        last = i == depth - 1
        g_final = wts["g_final"] if last else None
        if is_moe:
            router = (wts["w_router"][j], wts["b_router"][j], wts["n_experts"])
            h, hn, eg = _outproj(h, oa, ob, oc, wts["w_out"][i], wts["g_ffn"][i], router)
            rb = MOE_ROW_BLOCK if n >= 8 * MOE_ROW_BLOCK else LANES
            y = _moe(hn, eg, wts["w_exp_gate"], wts["w_exp_up"], wts["w_exp_down"], rb=rb,
                     n_experts=wts["n_experts"], moe_layer=j)
            h = _ple(h, y, eg, p_all, wts["g_ple"][i], wts["w_ple_gate"][i],
                     wts["w_ple_proj"][i], g_final, layer=i)
        else:
            h = _dense_tail(h, oa, ob, oc, wts["w_out"][i], wts["g_ffn"][i], wts["w_dense_gate"][j],
                            wts["w_dense_up"][j], wts["w_dense_down"][j], p_all, wts["g_ple"][i],
                            wts["w_ple_gate"][i], wts["w_ple_proj"][i], g_final, layer=i)

        new_lf.append(lf.reshape(batch, t, heads))
        new_ret.append(_unpair_states(sfin))

    y = h.reshape(batch, t, d)
    kv_shape = (depth, batch, t, heads, HEAD_DIM)
    return (y, kbuf.reshape(kv_shape), vbuf.reshape(kv_shape), jnp.stack(new_lf), jnp.stack(new_ret),
            jnp.stack(new_sgu) if has_past else None)


def kernel(x_prompt, x_sample, cache_fa_k, cache_fa_v, cache_fa_logf, state_ret, p_prompt, p_sample, g_mix, w_in, b_forget, sgu_ln_g, sgu_ln_b, sgu_w, sgu_b, ret_gn_g, w_out, g_ffn, w_dense_gate, w_dense_up, w_dense_down, w_router, b_router, w_exp_gate, w_exp_up, w_exp_down, g_ple, w_ple_gate, w_ple_proj, g_final):
    heads = cache_fa_k.shape[3]
    ret_heads = state_ret.shape[2]
    groups = sgu_w.shape[1]
    n_experts = w_router.shape[-1]
    dims = dict(heads=heads, ret_heads=ret_heads, faw=heads * HEAD_DIM, sw=groups * HEAD_DIM,
                rw=ret_heads * HEAD_DIM)
    assert cache_fa_k.shape[4] == HEAD_DIM and heads % 2 == 0 and ret_heads % 2 == 0

    w_in_b, offs = _prep_w_in(w_in, dims)
    row = lambda a: a[:, None, :].astype(F32)
    wr1 = w_router.astype(BF16)
    wr2 = (w_router - wr1.astype(F32)).astype(BF16)
    pad_e = lambda a: jnp.pad(a, [(0, 0)] * (a.ndim - 1) + [(0, LANES - n_experts)])
    wts = dict(
        offs=offs, n_experts=n_experts,
        g_mix=row(g_mix), w_in=w_in_b,
        b_forget=row(jnp.pad(b_forget, ((0, 0), (0, LANES - heads)))),
        sgu_ln_g=row(sgu_ln_g), sgu_ln_b=row(sgu_ln_b), sgu_w=sgu_w, sgu_b=sgu_b,
        ret_gn_g=row(ret_gn_g), w_out=w_out.astype(BF16), g_ffn=row(g_ffn),
        w_dense_gate=w_dense_gate.astype(BF16), w_dense_up=w_dense_up.astype(BF16),
        w_dense_down=w_dense_down.astype(BF16),
        w_router=jnp.concatenate([pad_e(wr1), pad_e(wr2)], axis=-1), b_router=row(pad_e(b_router)),
        w_exp_gate=w_exp_gate.astype(BF16).reshape((-1,) + w_exp_gate.shape[2:]),
        w_exp_up=w_exp_up.astype(BF16).reshape((-1,) + w_exp_up.shape[2:]),
        w_exp_down=w_exp_down.astype(BF16).reshape((-1,) + w_exp_down.shape[2:]),
        g_ple=row(g_ple), w_ple_gate=w_ple_gate.astype(BF16), w_ple_proj=w_ple_proj.astype(BF16),
        g_final=g_final[None, :].astype(F32),
    )

    y_p, k_p, v_p, lf_p, ret_p, _ = _trunk(x_prompt, p_prompt, None, wts, dims)
    y_s, k_s, v_s, lf_s, ret_s, sgu_s = _trunk(
        x_sample, p_sample, (cache_fa_k, cache_fa_v, cache_fa_logf, state_ret), wts, dims)
    return (y_p, y_s, k_p, v_p, lf_p, ret_p, k_s, v_s, lf_s, ret_s, sgu_s)
```
